```python
import jax, jax.numpy as jnp
from jax import lax
import numpy as np

D_MODEL = 1024
BATCH = 16
SEQ = 2048
DEPTH = 1

GRID_W = 64
NA_HEADS = 8
NA_HEAD_DIM = 64
NA_WIN_H_MAX = 8
NA_WIN_W = 16
NA_WIDTH = NA_HEADS * NA_HEAD_DIM
MLA_HEADS = 8
MLA_Q_RANK = 384
MLA_KV_RANK = 256
MLA_NOPE_DIM = 64
MLA_ROPE_DIM = 32
MLA_V_DIM = 64
MLA_WIDTH = MLA_HEADS * MLA_V_DIM
ROPE_BASE = 10000.0
Q_BLOCK = 128
MIX_WIDTH = NA_WIDTH + MLA_WIDTH
IN_PROJ_WIDTH = 3 * NA_WIDTH + MLA_Q_RANK + MLA_KV_RANK + MLA_ROPE_DIM
PEER_HEADS = 8
PEER_N_KEYS = 128
PEER_N_EXPERTS = PEER_N_KEYS * PEER_N_KEYS
PEER_KEY_DIM = 256
PEER_TOPK = 16
PEER_CHUNK = 128
PLE_DIM = 256
DN_ALPHA = float((2 * DEPTH) ** 0.25)
DN_BETA = float((8 * DEPTH) ** -0.25)
LN_EPS = 1e-5

kernel_name = "hybrid_na_mla_peer_deepnorm_block"


def layer_norm(x, g, b):
    xf = x.astype(jnp.float32)
    mu = jnp.mean(xf, axis=-1, keepdims=True)
    var = jnp.mean(jnp.square(xf - mu), axis=-1, keepdims=True)
    y = (xf - mu) * lax.rsqrt(var + LN_EPS)
    return (y * g.astype(jnp.float32) + b.astype(jnp.float32)).astype(x.dtype)


def rms_norm(x, g):
    xf = x.astype(jnp.float32)
    y = xf * lax.rsqrt(jnp.mean(jnp.square(xf), axis=-1, keepdims=True) + LN_EPS)
    return (y * g.astype(jnp.float32)).astype(x.dtype)


def rope_2d_tables(seq, dtype):
    t = jnp.arange(seq)
    row = (t // GRID_W).astype(jnp.float32)
    col = (t % GRID_W).astype(jnp.float32)
    axis_dim = MLA_ROPE_DIM // 2
    inv = ROPE_BASE ** (-jnp.arange(0, axis_dim, 2, dtype=jnp.float32) / axis_dim)
    ang = jnp.concatenate([row[:, None] * inv[None, :], col[:, None] * inv[None, :]], axis=-1)
    return jnp.cos(ang).astype(dtype), jnp.sin(ang).astype(dtype)


def apply_rope(x, cos, sin):
    xp = x.reshape(x.shape[:-1] + (MLA_ROPE_DIM // 2, 2))
    x1, x2 = xp[..., 0], xp[..., 1]
    out = jnp.stack([x1 * cos - x2 * sin, x1 * sin + x2 * cos], axis=-1)
    return out.reshape(x.shape)


def neighbourhood_attention(q, k, v, rpb):
    b, s, h, dh = q.shape
    rows = s // GRID_W
    kh = min(NA_WIN_H_MAX, rows)
    kw = NA_WIN_W
    qg = q.reshape(b, rows, GRID_W, h, dh)
    kg = k.reshape(b, rows, GRID_W, h, dh)
    vg = v.reshape(b, rows, GRID_W, h, dh)
    cols = jnp.arange(GRID_W)
    col_start = jnp.clip(cols - kw // 2, 0, GRID_W - kw)
    col_idx = col_start[:, None] + jnp.arange(kw)[None, :]
    dj = col_idx - cols[:, None] + (NA_WIN_W - 1)
    row_ids = jnp.arange(rows)
    row_start = jnp.clip(row_ids - kh // 2, 0, rows - kh)
    scale = dh ** -0.5

    def one_row(args):
        q_row, r, rs = args
        k_band = lax.dynamic_slice_in_dim(kg, rs, kh, axis=1)
        v_band = lax.dynamic_slice_in_dim(vg, rs, kh, axis=1)
        k_win = k_band[:, :, col_idx]
        v_win = v_band[:, :, col_idx]
        di = rs + jnp.arange(kh) - r + (NA_WIN_H_MAX - 1)
        bias = rpb[:, di][:, :, dj]
        bias = jnp.transpose(bias, (0, 2, 1, 3)).astype(jnp.float32)
        sc = jnp.einsum('bqhd,biqjhd->bhqij', q_row * scale, k_win).astype(jnp.float32) + bias[None]
        pr = jax.nn.softmax(sc.reshape(b, h, GRID_W, kh * kw), axis=-1)
        pr = pr.reshape(b, h, GRID_W, kh, kw).astype(v.dtype)
        return jnp.einsum('bhqij,biqjhd->bqhd', pr, v_win)

    out = lax.map(one_row, (jnp.moveaxis(qg, 1, 0), row_ids, row_start))
    return jnp.moveaxis(out, 0, 1).reshape(b, s, h * dh)


def latent_attention(c_q, c_kv, k_r, q_norm_g, kv_norm_g, w_uq, w_ukv, cos, sin):
    b, s, _ = c_kv.shape
    qd = MLA_NOPE_DIM + MLA_ROPE_DIM
    q = (rms_norm(c_q, q_norm_g) @ w_uq).reshape(b, s, MLA_HEADS, qd)
    q_nope, q_rope = q[..., :MLA_NOPE_DIM], q[..., MLA_NOPE_DIM:]
    q_rope = apply_rope(q_rope, cos[:, None, :], sin[:, None, :])
    kv = (rms_norm(c_kv, kv_norm_g) @ w_ukv).reshape(b, s, MLA_HEADS, MLA_NOPE_DIM + MLA_V_DIM)
    k_nope, v = kv[..., :MLA_NOPE_DIM], kv[..., MLA_NOPE_DIM:]
    k_rope = apply_rope(k_r, cos, sin)
    k_rope = jnp.broadcast_to(k_rope[:, :, None, :], (b, s, MLA_HEADS, MLA_ROPE_DIM))
    q = jnp.concatenate([q_nope, q_rope], axis=-1) * (qd ** -0.5)
    k = jnp.concatenate([k_nope, k_rope], axis=-1)
    nq = s // Q_BLOCK
    qb = jnp.moveaxis(q.reshape(b, nq, Q_BLOCK, MLA_HEADS, qd), 1, 0)

    def one_block(q_blk):
        sc = jnp.einsum('bqhd,bkhd->bhqk', q_blk, k).astype(jnp.float32)
        pr = jax.nn.softmax(sc, axis=-1).astype(v.dtype)
        return jnp.einsum('bhqk,bkhd->bqhd', pr, v)

    out = lax.map(one_block, qb)
    return jnp.moveaxis(out, 0, 1).reshape(b, s, MLA_WIDTH)


def peer_ffn(x, w_q, sub_keys, u_table, v_table):
    b, s, d = x.shape
    half = PEER_KEY_DIM // 2
    xc = x.reshape((b * s) // PEER_CHUNK, PEER_CHUNK, d)

    def one_chunk(xb):
        q = (xb @ w_q).reshape(PEER_CHUNK, PEER_HEADS, PEER_KEY_DIM)
        s1 = jnp.einsum('thd,kd->thk', q[..., :half], sub_keys[0])
        s2 = jnp.einsum('thd,kd->thk', q[..., half:], sub_keys[1])
        v1, i1 = lax.top_k(s1, PEER_TOPK)
        v2, i2 = lax.top_k(s2, PEER_TOPK)
        cand = (v1[..., :, None] + v2[..., None, :]).reshape(PEER_CHUNK, PEER_HEADS, PEER_TOPK * PEER_TOPK)
        cv, ci = lax.top_k(cand, PEER_TOPK)
        e1 = jnp.take_along_axis(i1, ci // PEER_TOPK, axis=-1)
        e2 = jnp.take_along_axis(i2, ci % PEER_TOPK, axis=-1)
        idx = (e1 * PEER_N_KEYS + e2).reshape(PEER_CHUNK, PEER_HEADS * PEER_TOPK)
        g = jax.nn.softmax(cv.astype(jnp.float32), axis=-1).reshape(PEER_CHUNK, -1).astype(xb.dtype)
        u = u_table[idx]
        hpre = jnp.einsum('td,ted->te', xb, u)
        act = g * jax.nn.gelu(hpre, approximate=False)
        return jnp.einsum('te,ted->td', act, v_table[idx])

    return lax.map(one_chunk, xc).reshape(b, s, d)


def setup_inputs(seed: int = 0) -> dict:
    key = jax.random.key(seed)
    ks = jax.random.split(key, 24)
    f32 = jnp.float32
    nrm = lambda k, shape, sc: jax.random.normal(k, shape, f32) * sc
    D = D_MODEL
    qd = MLA_NOPE_DIM + MLA_ROPE_DIM
    return {
        "x": nrm(ks[0], (BATCH, SEQ, D), 1.0),
        "p": nrm(ks[1], (DEPTH, BATCH, SEQ, PLE_DIM), 1.0),
        "emb_ln_g": 1.0 + nrm(ks[2], (D,), 0.02),
        "emb_ln_b": nrm(ks[3], (D,), 0.02),
        "w_in": nrm(ks[4], (DEPTH, D, IN_PROJ_WIDTH), D ** -0.5),
        "mla_q_norm_g": 1.0 + nrm(ks[5], (DEPTH, MLA_Q_RANK), 0.02),
        "mla_kv_norm_g": 1.0 + nrm(ks[6], (DEPTH, MLA_KV_RANK), 0.02),
        "w_uq": nrm(ks[7], (DEPTH, MLA_Q_RANK, MLA_HEADS * qd), MLA_Q_RANK ** -0.5),
        "w_ukv": nrm(ks[8], (DEPTH, MLA_KV_RANK, MLA_HEADS * (MLA_NOPE_DIM + MLA_V_DIM)), MLA_KV_RANK ** -0.5),
        "na_rpb": nrm(ks[9], (DEPTH, NA_HEADS, 2 * NA_WIN_H_MAX - 1, 2 * NA_WIN_W - 1), 0.5),
        "w_o": nrm(ks[10], (DEPTH, MIX_WIDTH, D), MIX_WIDTH ** -0.5 * DN_BETA),
        "ln1_g": 1.0 + nrm(ks[11], (DEPTH, D), 0.02),
        "ln1_b": nrm(ks[12], (DEPTH, D), 0.02),
        "peer_w_q": nrm(ks[13], (DEPTH, D, PEER_HEADS * PEER_KEY_DIM), D ** -0.5),
        "peer_sub_keys": nrm(ks[14], (DEPTH, 2, PEER_N_KEYS, PEER_KEY_DIM // 2), (PEER_KEY_DIM // 2) ** -0.5),
        "peer_u": nrm(ks[15], (DEPTH, PEER_N_EXPERTS, D), D ** -0.5),
        "peer_v": nrm(ks[16], (DEPTH, PEER_N_EXPERTS, D), DN_BETA * PEER_HEADS ** -0.5),
        "ple_w": nrm(ks[17], (DEPTH, PLE_DIM, D), PLE_DIM ** -0.5 * DN_BETA),
        "ple_gate_w": nrm(ks[18], (DEPTH, D, D), D ** -0.5),
        "ple_gate_b": nrm(ks[19], (DEPTH, D), 0.02),
        "ln2_g": 1.0 + nrm(ks[20], (DEPTH, D), 0.02),
        "ln2_b": nrm(ks[21], (DEPTH, D), 0.02),
    }


def reference(x, p, emb_ln_g, emb_ln_b, w_in, mla_q_norm_g, mla_kv_norm_g, w_uq, w_ukv,
              na_rpb, w_o, ln1_g, ln1_b, peer_w_q, peer_sub_keys, peer_u, peer_v,
              ple_w, ple_gate_w, ple_gate_b, ln2_g, ln2_b):
    b, s, _ = x.shape
    cos, sin = rope_2d_tables(s, x.dtype)
    split_points = list(np.cumsum([NA_WIDTH, NA_WIDTH, NA_WIDTH, MLA_Q_RANK, MLA_KV_RANK]))
    h = layer_norm(x, emb_ln_g, emb_ln_b)
    for i in range(DEPTH):
        z = h @ w_in[i]
        q_na, k_na, v_na, c_q, c_kv, k_r = jnp.split(z, split_points, axis=-1)
        hd = (b, s, NA_HEADS, NA_HEAD_DIM)
        a_na = neighbourhood_attention(q_na.reshape(hd), k_na.reshape(hd), v_na.reshape(hd), na_rpb[i])
        a_mla = latent_attention(c_q, c_kv, k_r, mla_q_norm_g[i], mla_kv_norm_g[i], w_uq[i], w_ukv[i], cos, sin)
        mix = jnp.concatenate([a_na, a_mla], axis=-1) @ w_o[i]
        h = layer_norm(DN_ALPHA * h + mix, ln1_g[i], ln1_b[i])
        ffn = peer_ffn(h, peer_w_q[i], peer_sub_keys[i], peer_u[i], peer_v[i])
        gate = jax.nn.sigmoid(h @ ple_gate_w[i] + ple_gate_b[i])
        ple = gate * (p[i] @ ple_w[i])
        h = layer_norm(DN_ALPHA * h + ffn + ple, ln2_g[i], ln2_b[i])
    return h
```

```python
import functools

import numpy as np
import jax
import jax.numpy as jnp
from jax import lax
from jax.experimental import pallas as pl
from jax.experimental.pallas import tpu as pltpu

F32 = jnp.float32
BF16 = jnp.bfloat16

D_MODEL = 1024
GRID_W = 64
NA_HEADS = 8
NA_HEAD_DIM = 64
NA_WIN_H = 8
NA_WIN_W = 16
NA_WIDTH = NA_HEADS * NA_HEAD_DIM
NA_BIAS_ROWS = 2 * NA_WIN_H - 1
NA_BIAS_COLS = 2 * NA_WIN_W - 1
MLA_HEADS = 8
MLA_Q_RANK = 384
MLA_KV_RANK = 256
MLA_NOPE = 64
MLA_ROPE = 32
MLA_V = 64
MLA_QD = MLA_NOPE + MLA_ROPE
ROPE_BASE = 10000.0
PEER_HEADS = 8
PEER_KEYS = 128
PEER_HALF = 128
PEER_TOPK = 16
PLE_DIM = 256
DEPTH = 1
DN_ALPHA = float((2 * DEPTH) ** 0.25)
LN_EPS = 1e-5
LANES = 128
NEG = -1e30
VMEM_LIMIT = 56 * 1024 * 1024

C_NA = 3 * NA_WIDTH
C_CQ = C_NA + MLA_Q_RANK
C_CKV = C_CQ + MLA_KV_RANK
C_KRA = C_CKV + LANES
C_IN = C_KRA + LANES

NT = (((1,), (1,)), ((), ()))


def _layer_norm(x, g, b):
    mu = jnp.mean(x, axis=-1, keepdims=True)
    xc = x - mu
    var = jnp.mean(xc * xc, axis=-1, keepdims=True)
    return xc * lax.rsqrt(var + LN_EPS) * g + b


def _rms_norm(x, g):
    return x * lax.rsqrt(jnp.mean(x * x, axis=-1, keepdims=True) + LN_EPS) * g


def _mm(a, b):
    return jnp.dot(a, b, preferred_element_type=F32)


def _mm_nt(a, b):
    return lax.dot_general(a, b, NT, preferred_element_type=F32)


def _params(*sem):
    return pltpu.CompilerParams(dimension_semantics=sem, vmem_limit_bytes=VMEM_LIMIT)


def _full(shape):
    return pl.BlockSpec(shape, lambda *_: (0,) * len(shape))


def _proj_body(x_ref, g0_ref, b0_ref, win_ref, qg_ref, kvg_ref, wqm_ref, wqs_ref, wk_ref, wv_ref,
               cos_ref, sin_ref, h_ref, na_ref, q_ref, k_ref, v_ref):
    h = _layer_norm(x_ref[...], g0_ref[...], b0_ref[...])
    h_ref[...] = h
    z = _mm(h.astype(BF16), win_ref[...])
    na_ref[:, :NA_WIDTH] = (z[:, :NA_WIDTH] * (NA_HEAD_DIM ** -0.5)).astype(BF16)
    na_ref[:, NA_WIDTH:] = z[:, NA_WIDTH:C_NA].astype(BF16)
    cqn = _rms_norm(z[:, C_NA:C_CQ], qg_ref[...]).astype(BF16)
    ckvn = _rms_norm(z[:, C_CQ:C_CKV], kvg_ref[...]).astype(BF16)
    cos = cos_ref[...]
    sin = sin_ref[...]
    q = _mm(cqn, wqm_ref[...]) * jnp.tile(cos, (1, MLA_HEADS)) + _mm(cqn, wqs_ref[...]) * jnp.tile(sin, (1, MLA_HEADS))
    q_ref[...] = (q * (MLA_QD ** -0.5)).astype(BF16)
    k_rot = z[:, C_CKV:C_KRA] * cos + z[:, C_KRA:C_IN] * sin
    k_ref[...] = (_mm(ckvn, wk_ref[...]) + jnp.tile(k_rot, (1, MLA_HEADS))).astype(BF16)
    v_ref[...] = _mm(ckvn, wv_ref[...]).astype(BF16)


def _proj(x2, g0, b0, w_in_ext, qg, kvg, wq_main, wq_sw, wk_pad, wv_pad, cos128, sin128, seq, tm):
    t = x2.shape[0]
    n_pos = seq // tm
    row = lambda w: pl.BlockSpec((tm, w), lambda i: (i, 0))
    pos = pl.BlockSpec((tm, LANES), lambda i: (i % n_pos, 0))
    hp = MLA_HEADS * LANES
    return pl.pallas_call(
        _proj_body,
        grid=(t // tm,),
        in_specs=[row(D_MODEL), _full((1, D_MODEL)), _full((1, D_MODEL)), _full((D_MODEL, C_IN)),
                  _full((1, MLA_Q_RANK)), _full((1, MLA_KV_RANK)), _full((MLA_Q_RANK, hp)), _full((MLA_Q_RANK, hp)),
                  _full((MLA_KV_RANK, hp)), _full((MLA_KV_RANK, hp)), pos, pos],
        out_specs=[row(D_MODEL), row(C_NA), row(hp), row(hp), row(hp)],
        out_shape=[jax.ShapeDtypeStruct((t, D_MODEL), F32), jax.ShapeDtypeStruct((t, C_NA), BF16),
                   jax.ShapeDtypeStruct((t, hp), BF16), jax.ShapeDtypeStruct((t, hp), BF16),
                   jax.ShapeDtypeStruct((t, hp), BF16)],
        compiler_params=_params("parallel"),
        name="proj",
    )(x2, g0, b0, w_in_ext, qg, kvg, wq_main, wq_sw, wk_pad, wv_pad, cos128, sin128)


def _na_bias_body(rpb_ref, o_ref):
    hh = pl.program_id(0)
    lane = lax.broadcasted_iota(jnp.int32, (GRID_W, LANES), 1)
    qc = lax.broadcasted_iota(jnp.int32, (GRID_W, LANES), 0)
    kc = lane % GRID_W
    first = lane < GRID_W
    dj = kc - qc + (NA_WIN_W - 1)
    cs = jnp.clip(qc - NA_WIN_W // 2, 0, GRID_W - NA_WIN_W)
    valid = (kc >= cs) & (kc < cs + NA_WIN_W)
    pair = []
    for a in range(NA_BIAS_ROWS - 1):
        acc = jnp.full((GRID_W, LANES), NEG, F32)
        for d in range(NA_BIAS_COLS):
            lo = rpb_ref[(hh * NA_BIAS_ROWS + a) * NA_BIAS_COLS + d]
            hi = rpb_ref[(hh * NA_BIAS_ROWS + a + 1) * NA_BIAS_COLS + d]
            acc = jnp.where(valid & (dj == d), jnp.where(first, lo, hi), acc)
        pair.append(acc)
    for d0 in range(NA_WIN_H):
        o_ref[0, d0] = jnp.concatenate([pair[d0 + 2 * i] for i in range(NA_WIN_H // 2)], axis=1)


def _na_bias(rpb_flat):
    band = NA_WIN_H * GRID_W
    return pl.pallas_call(
        _na_bias_body,
        grid=(NA_HEADS,),
        in_specs=[pl.BlockSpec(memory_space=pltpu.SMEM)],
        out_specs=pl.BlockSpec((1, NA_WIN_H, GRID_W, band), lambda h: (h, 0, 0, 0)),
        out_shape=jax.ShapeDtypeStruct((NA_HEADS, NA_WIN_H, GRID_W, band), F32),
        compiler_params=_params("arbitrary"),
        name="na_bias",
    )(rpb_flat)


def _na_body(q_ref, k_ref, v_ref, bias_ref, o_ref, *, rows):
    first = lax.broadcasted_iota(jnp.int32, (GRID_W, LANES), 1) < NA_HEAD_DIM
    band = NA_WIN_H * GRID_W

    def one_row(r, carry):
        rs = jnp.clip(r - NA_WIN_H // 2, 0, rows - NA_WIN_H)
        d0 = rs - r + (NA_WIN_H - 1)
        q0 = pl.multiple_of(r * GRID_W, GRID_W)
        k0 = pl.multiple_of(rs * GRID_W, GRID_W)
        for j in range(NA_HEADS // 2):
            cols = slice(j * LANES, (j + 1) * LANES)
            qp = q_ref[0, pl.ds(q0, GRID_W), cols]
            kb = k_ref[0, pl.ds(k0, band), cols]
            vb = v_ref[0, pl.ds(k0, band), cols]
            outs = []
            for e in range(2):
                qm = jnp.where(first if e == 0 else jnp.logical_not(first), qp, jnp.zeros_like(qp))
                s = _mm_nt(qm, kb) + bias_ref[2 * j + e, d0]
                m = jnp.max(s, axis=-1, keepdims=True)
                p = jnp.exp(s - m)
                l = jnp.sum(p, axis=-1, keepdims=True)
                outs.append(_mm(p.astype(BF16), vb) / l)
            o_ref[0, pl.ds(q0, GRID_W), cols] = jnp.where(first, outs[0], outs[1]).astype(BF16)
        return carry

    lax.fori_loop(0, rows, one_row, 0)


def _na_attn(qkv, bias, seq):
    b = qkv.shape[0]
    rows = seq // GRID_W
    part = lambda c: pl.BlockSpec((1, seq, NA_WIDTH), lambda i: (i, 0, c))
    return pl.pallas_call(
        functools.partial(_na_body, rows=rows),
        grid=(b,),
        in_specs=[part(0), part(1), part(2), _full(bias.shape)],
        out_specs=pl.BlockSpec((1, seq, NA_WIDTH), lambda i: (i, 0, 0)),
        out_shape=jax.ShapeDtypeStruct((b, seq, NA_WIDTH), BF16),
        compiler_params=_params("parallel"),
        name="na_attn",
    )(qkv, qkv, qkv, bias)


def _mla_body(q_ref, k_ref, v_ref, o_ref):
    for j in range(MLA_HEADS // 2):
        acc = None
        for e in range(2):
            cols = slice((2 * j + e) * LANES, (2 * j + e + 1) * LANES)
            s = _mm_nt(q_ref[0, :, cols], k_ref[0, :, cols])
            m = jnp.max(s, axis=-1, keepdims=True)
            p = jnp.exp(s - m)
            l = jnp.sum(p, axis=-1, keepdims=True)
            o = _mm(p.astype(BF16), v_ref[0, :, cols]) / l
            acc = o if acc is None else acc + o
        o_ref[0, :, j * LANES:(j + 1) * LANES] = acc.astype(BF16)


def _mla_attn(q, k, v, tq):
    b, seq, hp = q.shape
    kv = pl.BlockSpec((1, seq, hp), lambda i, j: (i, 0, 0))
    width = MLA_HEADS * MLA_V
    return pl.pallas_call(
        _mla_body,
        grid=(b, seq // tq),
        in_specs=[pl.BlockSpec((1, tq, hp), lambda i, j: (i, j, 0)), kv, kv],
        out_specs=pl.BlockSpec((1, tq, width), lambda i, j: (i, j, 0)),
        out_shape=jax.ShapeDtypeStruct((b, seq, width), BF16),
        compiler_params=_params("parallel", "arbitrary"),
        name="mla_attn",
    )(q, k, v)


def _mix_body(h_ref, na_ref, mla_ref, p_ref, wo_ref, g1_ref, b1_ref, wg_ref, bg_ref, wple_ref, wqt_ref, keys_ref,
              h1b_ref, r2_ref, sc_ref):
    mix = _mm(na_ref[...], wo_ref[:NA_WIDTH, :]) + _mm(mla_ref[...], wo_ref[NA_WIDTH:, :])
    h1 = _layer_norm(DN_ALPHA * h_ref[...] + mix, g1_ref[...], b1_ref[...])
    h1b = h1.astype(BF16)
    h1b_ref[...] = h1b
    gate = jax.nn.sigmoid(_mm(h1b, wg_ref[...]) + bg_ref[...])
    ple = gate * _mm(p_ref[...].astype(BF16), wple_ref[...])
    r2_ref[...] = DN_ALPHA * h1 + ple
    q_t = _mm_nt(wqt_ref[...], h1b).astype(BF16)
    for blk in range(2 * PEER_HEADS):
        rows = slice(blk * PEER_HALF, (blk + 1) * PEER_HALF)
        sc_ref[rows, :] = _mm(keys_ref[blk % 2], q_t[rows, :])


def _mix(h, a_na, a_mla, p2, w_o, g1, b1, w_g, b_g, w_ple, w_qt, keys, tm):
    t = h.shape[0]
    row = lambda w: pl.BlockSpec((tm, w), lambda i: (i, 0))
    nq = 2 * PEER_HEADS * PEER_HALF
    return pl.pallas_call(
        _mix_body,
        grid=(t // tm,),
        in_specs=[row(D_MODEL), row(NA_WIDTH), row(MLA_HEADS * MLA_V), row(PLE_DIM), _full((D_MODEL, D_MODEL)),
                  _full((1, D_MODEL)), _full((1, D_MODEL)), _full((D_MODEL, D_MODEL)), _full((1, D_MODEL)),
                  _full((PLE_DIM, D_MODEL)), _full((nq, D_MODEL)), _full((2, PEER_KEYS, PEER_HALF))],
        out_specs=[row(D_MODEL), row(D_MODEL), pl.BlockSpec((nq, tm), lambda i: (0, i))],
        out_shape=[jax.ShapeDtypeStruct((t, D_MODEL), BF16), jax.ShapeDtypeStruct((t, D_MODEL), F32),
                   jax.ShapeDtypeStruct((nq, t), F32)],
        compiler_params=_params("parallel"),
        name="mix",
    )(h, a_na, a_mla, p2, w_o, g1, b1, w_g, b_g, w_ple, w_qt, keys)


def _top16(s, tag):
    big = jnp.int32(1 << 20)
    vals, tags = [], []
    for _ in range(PEER_TOPK):
        m = jnp.max(s, axis=0, keepdims=True)
        sel = jnp.min(jnp.where(s == m, tag, big), axis=0, keepdims=True)
        vals.append(m)
        tags.append(sel)
        s = jnp.where(tag == sel, -jnp.inf, s)
    return jnp.concatenate(vals, axis=0), jnp.concatenate(tags, axis=0)


_HALF_RANKS = PEER_TOPK // 2


def _topk_body(sc_ref, e1_ref, e2_ref, g_ref):
    tm = sc_ref.shape[1]
    key_id = lax.broadcasted_iota(jnp.int32, (PEER_KEYS, tm), 0)
    r16 = lax.broadcasted_iota(jnp.int32, (PEER_TOPK, tm), 0)
    r8 = lax.broadcasted_iota(jnp.int32, (_HALF_RANKS, tm), 0)
    flat = jnp.concatenate([r16] + [i * PEER_TOPK + r8 for i in range(1, _HALF_RANKS)]
                           + [(r8 + _HALF_RANKS) * PEER_TOPK], axis=0)
    e1s, e2s, gs = [], [], []
    for hh in range(PEER_HEADS):
        base = hh * 2 * PEER_HALF
        v1, i1 = _top16(sc_ref[base:base + PEER_HALF, :], key_id)
        v2, i2 = _top16(sc_ref[base + PEER_HALF:base + 2 * PEER_HALF, :], key_id)
        cand = jnp.concatenate(
            [v1[0:1, :] + v2] + [v1[i:i + 1, :] + v2[:_HALF_RANKS, :] for i in range(1, _HALF_RANKS)]
            + [v1[_HALF_RANKS:, :] + v2[0:1, :]], axis=0)
        cv, ci = _top16(cand, flat)
        ci1 = lax.shift_right_logical(ci, 4)
        ci2 = lax.bitwise_and(ci, PEER_TOPK - 1)
        e1 = jnp.zeros((PEER_TOPK, tm), jnp.int32)
        e2 = jnp.zeros((PEER_TOPK, tm), jnp.int32)
        for r in range(PEER_TOPK):
            e1 = jnp.where(ci1 == r, i1[r:r + 1, :], e1)
            e2 = jnp.where(ci2 == r, i2[r:r + 1, :], e2)
        p = jnp.exp(cv - jnp.max(cv, axis=0, keepdims=True))
        gs.append(p / jnp.sum(p, axis=0, keepdims=True))
        e1s.append(e1)
        e2s.append(e2)
    e1_ref[...] = jnp.concatenate(e1s, axis=0).T
    e2_ref[...] = jnp.concatenate(e2s, axis=0).T
    g_ref[...] = jnp.concatenate(gs, axis=0).T


def _topk(scores_t, tm):
    nq, t = scores_t.shape
    npair = PEER_HEADS * PEER_TOPK
    out = pl.BlockSpec((tm, npair), lambda i: (i, 0))
    return pl.pallas_call(
        _topk_body,
        grid=(t // tm,),
        in_specs=[pl.BlockSpec((nq, tm), lambda i: (0, i))],
        out_specs=[out, out, out],
        out_shape=[jax.ShapeDtypeStruct((t, npair), jnp.int32), jax.ShapeDtypeStruct((t, npair), jnp.int32),
                   jax.ShapeDtypeStruct((t, npair), F32)],
        compiler_params=_params("parallel"),
        name="topk",
    )(scores_t)


PEER_CHUNK = 8


def _peer_body(x_ref, e1_ref, e2_ref, g_ref, u_ref, v_ref, r2_ref, g2_ref, b2_ref, o_ref, wall_ref, acc_ref):
    j = pl.program_id(1)
    tm = x_ref.shape[0]
    eb = u_ref.shape[0]
    npair = e1_ref.shape[1]

    @pl.when(j == 0)
    def _():
        acc_ref[...] = jnp.zeros_like(acc_ref)
        key_id = lax.broadcasted_iota(jnp.int32, (PEER_CHUNK, PEER_KEYS, 2 * npair), 1)

        def chunk(c, carry):
            rows = pl.ds(pl.multiple_of(c * PEER_CHUNK, PEER_CHUNK), PEER_CHUNK)
            e1 = e1_ref[rows, :]
            e2 = e2_ref[rows, :]
            g = g_ref[rows, :]
            g_hi = g.astype(BF16).astype(F32)
            gx = jnp.concatenate([g_hi, g - g_hi], axis=1)[:, None, :]
            e1x = jnp.concatenate([e1, e1], axis=1)[:, None, :]
            e2x = jnp.concatenate([e2, e2], axis=1)[:, None, :]
            a = jnp.where(e1x == key_id, 1.0, 0.0).astype(BF16)
            bm = jnp.where(e2x == key_id, gx, 0.0).astype(BF16)
            wall_ref[rows] = jnp.einsum("tep,tfp->tef", a, bm, preferred_element_type=F32)
            return carry

        lax.fori_loop(0, tm // PEER_CHUNK, chunk, 0)

    nsub = eb // PEER_KEYS
    hpre = _mm_nt(x_ref[...], u_ref[...])
    w = jnp.concatenate([wall_ref[:, j * nsub + c, :] for c in range(nsub)], axis=1)
    act = w * (0.5 * hpre * (1.0 + lax.erf(hpre * (2.0 ** -0.5))))
    acc_ref[...] += _mm(act.astype(BF16), v_ref[...])

    @pl.when(j == pl.num_programs(1) - 1)
    def _():
        o_ref[...] = _layer_norm(r2_ref[...] + acc_ref[...], g2_ref[...], b2_ref[...])


def _peer(h1b, e1, e2, g, u, v, r2, g2, b2, tm, eb):
    t = h1b.shape[0]
    n_exp = u.shape[0]
    npair = e1.shape[1]
    row = lambda w: pl.BlockSpec((tm, w), lambda i, j: (i, 0))
    tab = pl.BlockSpec((eb, D_MODEL), lambda i, j: (j, 0))
    vec = pl.BlockSpec((1, D_MODEL), lambda i, j: (0, 0))
    return pl.pallas_call(
        _peer_body,
        grid=(t // tm, n_exp // eb),
        in_specs=[row(D_MODEL), row(npair), row(npair), row(npair), tab, tab, row(D_MODEL), vec, vec],
        out_specs=row(D_MODEL),
        out_shape=jax.ShapeDtypeStruct((t, D_MODEL), F32),
        scratch_shapes=[pltpu.VMEM((tm, PEER_KEYS, PEER_KEYS), F32), pltpu.VMEM((tm, D_MODEL), F32)],
        compiler_params=_params("parallel", "arbitrary"),
        name="peer",
    )(h1b, e1, e2, g, u, v, r2, g2, b2)


def _rope_tables(seq):
    t = jnp.arange(seq)
    row = (t // GRID_W).astype(F32)
    col = (t % GRID_W).astype(F32)
    axis_dim = MLA_ROPE // 2
    inv = ROPE_BASE ** (-jnp.arange(0, axis_dim, 2, dtype=F32) / axis_dim)
    ang = jnp.concatenate([row[:, None] * inv[None, :], col[:, None] * inv[None, :]], axis=-1)
    cos, sin = jnp.cos(ang), jnp.sin(ang)
    pad = LANES - MLA_NOPE - MLA_ROPE
    cos128 = jnp.concatenate([jnp.ones((seq, MLA_NOPE), F32), cos, cos, jnp.zeros((seq, pad), F32)], axis=1)
    sin128 = jnp.concatenate([jnp.zeros((seq, MLA_NOPE), F32), -sin, sin, jnp.zeros((seq, pad), F32)], axis=1)
    return cos128, sin128


_PERM = np.concatenate([np.arange(0, MLA_ROPE, 2), np.arange(1, MLA_ROPE, 2)])
_PERM_SW = np.concatenate([np.arange(1, MLA_ROPE, 2), np.arange(0, MLA_ROPE, 2)])


def _prep_weights(w_in, w_uq, w_ukv):
    pad = LANES - MLA_NOPE - MLA_ROPE
    zin = lambda n: jnp.zeros((D_MODEL, n), F32)
    kr = w_in[:, C_CKV:]
    w_in_ext = jnp.concatenate(
        [w_in[:, :C_CKV], zin(MLA_NOPE), kr[:, _PERM], zin(pad), zin(MLA_NOPE), kr[:, _PERM_SW], zin(pad)], axis=1)
    wq = w_uq.reshape(MLA_Q_RANK, MLA_HEADS, MLA_QD)
    zq = lambda n: jnp.zeros((MLA_Q_RANK, MLA_HEADS, n), F32)
    rope = wq[:, :, MLA_NOPE:]
    wq_main = jnp.concatenate([wq[:, :, :MLA_NOPE], rope[:, :, _PERM], zq(pad)], axis=2)
    wq_sw = jnp.concatenate([zq(MLA_NOPE), rope[:, :, _PERM_SW], zq(pad)], axis=2)
    wkv = w_ukv.reshape(MLA_KV_RANK, MLA_HEADS, MLA_NOPE + MLA_V)
    zk = jnp.zeros((MLA_KV_RANK, MLA_HEADS, LANES - MLA_NOPE), F32)
    wk_pad = jnp.concatenate([wkv[:, :, :MLA_NOPE], zk], axis=2)
    vv = wkv[:, :, MLA_NOPE:]
    zv = jnp.zeros_like(vv)
    odd = (jnp.arange(MLA_HEADS) % 2 == 1)[None, :, None]
    wv_pad = jnp.where(odd, jnp.concatenate([zv, vv], axis=2), jnp.concatenate([vv, zv], axis=2))
    hp = MLA_HEADS * LANES
    flat = lambda w: w.reshape(w.shape[0], hp).astype(BF16)
    return w_in_ext.astype(BF16), flat(wq_main), flat(wq_sw), flat(wk_pad), flat(wv_pad)


def kernel(x, p, emb_ln_g, emb_ln_b, w_in, mla_q_norm_g, mla_kv_norm_g, w_uq, w_ukv, na_rpb, w_o, ln1_g, ln1_b,
           peer_w_q, peer_sub_keys, peer_u, peer_v, ple_w, ple_gate_w, ple_gate_b, ln2_g, ln2_b):
    b, seq, d = x.shape
    assert d == D_MODEL and seq % GRID_W == 0 and w_in.shape[0] == DEPTH
    t = b * seq
    vec = lambda a: a.reshape(1, -1).astype(F32)
    w_in_ext, wq_main, wq_sw, wk_pad, wv_pad = _prep_weights(w_in[0], w_uq[0], w_ukv[0])
    cos128, sin128 = _rope_tables(seq)

    h, qkv_na, q_mla, k_mla, v_mla = _proj(
        x.reshape(t, d), vec(emb_ln_g), vec(emb_ln_b), w_in_ext, vec(mla_q_norm_g[0]), vec(mla_kv_norm_g[0]),
        wq_main, wq_sw, wk_pad, wv_pad, cos128, sin128, seq, tm=256)

    bias = _na_bias(na_rpb[0].reshape(-1).astype(F32))
    a_na = _na_attn(qkv_na.reshape(b, seq, C_NA), bias, seq)
    hp = MLA_HEADS * LANES
    a_mla = _mla_attn(q_mla.reshape(b, seq, hp), k_mla.reshape(b, seq, hp), v_mla.reshape(b, seq, hp), tq=512)

    h1b, r2, scores_t = _mix(
        h, a_na.reshape(t, NA_WIDTH), a_mla.reshape(t, MLA_HEADS * MLA_V), p[0].reshape(t, PLE_DIM),
        w_o[0].astype(BF16), vec(ln1_g[0]), vec(ln1_b[0]), ple_gate_w[0].astype(BF16), vec(ple_gate_b[0]),
        ple_w[0].astype(BF16), peer_w_q[0].T.astype(BF16), peer_sub_keys[0].astype(BF16), tm=256)

    e1, e2, gates = _topk(scores_t, tm=256)
    out = _peer(h1b, e1, e2, gates, peer_u[0].astype(BF16), peer_v[0].astype(BF16), r2,
                vec(ln2_g[0]), vec(ln2_b[0]), tm=256, eb=1024)
    return out.reshape(b, seq, d)
```

```python
import functools

import numpy as np
import jax
import jax.numpy as jnp
from jax import lax
from jax.experimental import pallas as pl
from jax.experimental.pallas import tpu as pltpu

F32 = jnp.float32
BF16 = jnp.bfloat16

D_MODEL = 1024
GRID_W = 64
NA_HEADS = 8
NA_HEAD_DIM = 64
NA_WIN_H = 8
NA_WIN_W = 16
NA_WIDTH = NA_HEADS * NA_HEAD_DIM
NA_BIAS_ROWS = 2 * NA_WIN_H - 1
NA_BIAS_COLS = 2 * NA_WIN_W - 1
MLA_HEADS = 8
MLA_Q_RANK = 384
MLA_KV_RANK = 256
MLA_NOPE = 64
MLA_ROPE = 32
MLA_V = 64
MLA_QD = MLA_NOPE + MLA_ROPE
ROPE_BASE = 10000.0
PEER_HEADS = 8
PEER_KEYS = 128
PEER_HALF = 128
PEER_TOPK = 16
PLE_DIM = 256
DEPTH = 1
DN_ALPHA = float((2 * DEPTH) ** 0.25)
LN_EPS = 1e-5
LANES = 128
NEG = -1e30
VMEM_LIMIT = 56 * 1024 * 1024

C_NA = 3 * NA_WIDTH
C_CQ = C_NA + MLA_Q_RANK
C_CKV = C_CQ + MLA_KV_RANK
C_KRA = C_CKV + LANES
C_IN = C_KRA + LANES

NT = (((1,), (1,)), ((), ()))


def _layer_norm(x, g, b):
    mu = jnp.mean(x, axis=-1, keepdims=True)
    xc = x - mu
    var = jnp.mean(xc * xc, axis=-1, keepdims=True)
    return xc * lax.rsqrt(var + LN_EPS) * g + b


def _rms_norm(x, g):
    return x * lax.rsqrt(jnp.mean(x * x, axis=-1, keepdims=True) + LN_EPS) * g


def _mm(a, b):
    return jnp.dot(a, b, preferred_element_type=F32)


def _mm_nt(a, b):
    return lax.dot_general(a, b, NT, preferred_element_type=F32)


def _params(*sem):
    return pltpu.CompilerParams(dimension_semantics=sem, vmem_limit_bytes=VMEM_LIMIT)


def _full(shape):
    return pl.BlockSpec(shape, lambda *_: (0,) * len(shape))


def _proj_body(x_ref, g0_ref, b0_ref, win_ref, qg_ref, kvg_ref, wqm_ref, wqs_ref, wk_ref, wv_ref,
               cos_ref, sin_ref, h_ref, na_ref, q_ref, k_ref, v_ref):
    h = _layer_norm(x_ref[...], g0_ref[...], b0_ref[...])
    h_ref[...] = h
    z = _mm(h.astype(BF16), win_ref[...])
    na_ref[:, :NA_WIDTH] = (z[:, :NA_WIDTH] * (NA_HEAD_DIM ** -0.5)).astype(BF16)
    na_ref[:, NA_WIDTH:] = z[:, NA_WIDTH:C_NA].astype(BF16)
    cqn = _rms_norm(z[:, C_NA:C_CQ], qg_ref[...]).astype(BF16)
    ckvn = _rms_norm(z[:, C_CQ:C_CKV], kvg_ref[...]).astype(BF16)
    cos = cos_ref[...]
    sin = sin_ref[...]
    q = _mm(cqn, wqm_ref[...]) * jnp.tile(cos, (1, MLA_HEADS)) + _mm(cqn, wqs_ref[...]) * jnp.tile(sin, (1, MLA_HEADS))
    q_ref[...] = (q * (MLA_QD ** -0.5)).astype(BF16)
    k_rot = z[:, C_CKV:C_KRA] * cos + z[:, C_KRA:C_IN] * sin
    k_ref[...] = (_mm(ckvn, wk_ref[...]) + jnp.tile(k_rot, (1, MLA_HEADS))).astype(BF16)
    v_ref[...] = _mm(ckvn, wv_ref[...]).astype(BF16)


def _proj(x2, g0, b0, w_in_ext, qg, kvg, wq_main, wq_sw, wk_pad, wv_pad, cos128, sin128, seq, tm):
    t = x2.shape[0]
    n_pos = seq // tm
    row = lambda w: pl.BlockSpec((tm, w), lambda i: (i, 0))
    pos = pl.BlockSpec((tm, LANES), lambda i: (i % n_pos, 0))
    hp = MLA_HEADS * LANES
    return pl.pallas_call(
        _proj_body,
        grid=(t // tm,),
        in_specs=[row(D_MODEL), _full((1, D_MODEL)), _full((1, D_MODEL)), _full((D_MODEL, C_IN)),
                  _full((1, MLA_Q_RANK)), _full((1, MLA_KV_RANK)), _full((MLA_Q_RANK, hp)), _full((MLA_Q_RANK, hp)),
                  _full((MLA_KV_RANK, hp)), _full((MLA_KV_RANK, hp)), pos, pos],
        out_specs=[row(D_MODEL), row(C_NA), row(hp), row(hp), row(hp)],
        out_shape=[jax.ShapeDtypeStruct((t, D_MODEL), F32), jax.ShapeDtypeStruct((t, C_NA), BF16),
                   jax.ShapeDtypeStruct((t, hp), BF16), jax.ShapeDtypeStruct((t, hp), BF16),
                   jax.ShapeDtypeStruct((t, hp), BF16)],
        compiler_params=_params("parallel"),
        name="proj",
    )(x2, g0, b0, w_in_ext, qg, kvg, wq_main, wq_sw, wk_pad, wv_pad, cos128, sin128)


def _na_bias_body(rpb_ref, o_ref):
    hh = pl.program_id(0)
    lane = lax.broadcasted_iota(jnp.int32, (GRID_W, LANES), 1)
    qc = lax.broadcasted_iota(jnp.int32, (GRID_W, LANES), 0)
    kc = lane % GRID_W
    first = lane < GRID_W
    dj = kc - qc + (NA_WIN_W - 1)
    cs = jnp.clip(qc - NA_WIN_W // 2, 0, GRID_W - NA_WIN_W)
    valid = (kc >= cs) & (kc < cs + NA_WIN_W)
    pair = []
    for a in range(NA_BIAS_ROWS - 1):
        acc = jnp.full((GRID_W, LANES), NEG, F32)
        for d in range(NA_BIAS_COLS):
            lo = rpb_ref[(hh * NA_BIAS_ROWS + a) * NA_BIAS_COLS + d]
            hi = rpb_ref[(hh * NA_BIAS_ROWS + a + 1) * NA_BIAS_COLS + d]
            acc = jnp.where(valid & (dj == d), jnp.where(first, lo, hi), acc)
        pair.append(acc)
    for d0 in range(NA_WIN_H):
        o_ref[0, d0] = jnp.concatenate([pair[d0 + 2 * i] for i in range(NA_WIN_H // 2)], axis=1)


def _na_bias(rpb_flat):
    band = NA_WIN_H * GRID_W
    return pl.pallas_call(
        _na_bias_body,
        grid=(NA_HEADS,),
        in_specs=[pl.BlockSpec(memory_space=pltpu.SMEM)],
        out_specs=pl.BlockSpec((1, NA_WIN_H, GRID_W, band), lambda h: (h, 0, 0, 0)),
        out_shape=jax.ShapeDtypeStruct((NA_HEADS, NA_WIN_H, GRID_W, band), F32),
        compiler_params=_params("arbitrary"),
        name="na_bias",
    )(rpb_flat)


NA_ROW_UNROLL = 2


def _na_body(q_ref, k_ref, v_ref, bias_ref, o_ref, *, rows):
    first = lax.broadcasted_iota(jnp.int32, (GRID_W, LANES), 1) < NA_HEAD_DIM
    band = NA_WIN_H * GRID_W

    def one_row(r, carry):
        rs = jnp.clip(r - NA_WIN_H // 2, 0, rows - NA_WIN_H)
        d0 = rs - r + (NA_WIN_H - 1)
        q0 = pl.multiple_of(r * GRID_W, GRID_W)
        k0 = pl.multiple_of(rs * GRID_W, GRID_W)
        pairs = range(NA_HEADS // 2)
        scores = []
        for j in pairs:
            cols = slice(j * LANES, (j + 1) * LANES)
            qp = q_ref[0, pl.ds(q0, GRID_W), cols]
            zero = jnp.zeros_like(qp)
            qs = jnp.concatenate([jnp.where(first, qp, zero), jnp.where(first, zero, qp)], axis=0)
            bias = jnp.concatenate([bias_ref[2 * j, d0], bias_ref[2 * j + 1, d0]], axis=0)
            scores.append(_mm_nt(qs, k_ref[0, pl.ds(k0, band), cols]) + bias)
        probs, sums = [], []
        for s in scores:
            p = jnp.exp(s - jnp.max(s, axis=-1, keepdims=True))
            sums.append(jnp.sum(p, axis=-1, keepdims=True))
            probs.append(p.astype(BF16))
        for j in pairs:
            cols = slice(j * LANES, (j + 1) * LANES)
            o = _mm(probs[j], v_ref[0, pl.ds(k0, band), cols]) / sums[j]
            o_ref[0, pl.ds(q0, GRID_W), cols] = jnp.where(first, o[:GRID_W], o[GRID_W:]).astype(BF16)
        return carry

    lax.fori_loop(0, rows, one_row, 0, unroll=NA_ROW_UNROLL)


def _na_attn(qkv, bias, seq):
    b = qkv.shape[0]
    rows = seq // GRID_W
    part = lambda c: pl.BlockSpec((1, seq, NA_WIDTH), lambda i: (i, 0, c))
    return pl.pallas_call(
        functools.partial(_na_body, rows=rows),
        grid=(b,),
        in_specs=[part(0), part(1), part(2), _full(bias.shape)],
        out_specs=pl.BlockSpec((1, seq, NA_WIDTH), lambda i: (i, 0, 0)),
        out_shape=jax.ShapeDtypeStruct((b, seq, NA_WIDTH), BF16),
        compiler_params=_params("parallel"),
        name="na_attn",
    )(qkv, qkv, qkv, bias)


def _mla_body(q_ref, k_ref, v_ref, o_ref):
    for j in range(MLA_HEADS // 2):
        acc = None
        for e in range(2):
            cols = slice((2 * j + e) * LANES, (2 * j + e + 1) * LANES)
            s = _mm_nt(q_ref[0, :, cols], k_ref[0, :, cols])
            m = jnp.max(s, axis=-1, keepdims=True)
            p = jnp.exp(s - m)
            l = jnp.sum(p, axis=-1, keepdims=True)
            o = _mm(p.astype(BF16), v_ref[0, :, cols]) / l
            acc = o if acc is None else acc + o
        o_ref[0, :, j * LANES:(j + 1) * LANES] = acc.astype(BF16)


def _mla_attn(q, k, v, tq):
    b, seq, hp = q.shape
    kv = pl.BlockSpec((1, seq, hp), lambda i, j: (i, 0, 0))
    width = MLA_HEADS * MLA_V
    return pl.pallas_call(
        _mla_body,
        grid=(b, seq // tq),
        in_specs=[pl.BlockSpec((1, tq, hp), lambda i, j: (i, j, 0)), kv, kv],
        out_specs=pl.BlockSpec((1, tq, width), lambda i, j: (i, j, 0)),
        out_shape=jax.ShapeDtypeStruct((b, seq, width), BF16),
        compiler_params=_params("parallel", "arbitrary"),
        name="mla_attn",
    )(q, k, v)


def _mix_body(h_ref, na_ref, mla_ref, p_ref, wo_ref, g1_ref, b1_ref, wg_ref, bg_ref, wple_ref, wqt_ref, keys_ref,
              h1b_ref, r2_ref, sc_ref):
    mix = _mm(na_ref[...], wo_ref[:NA_WIDTH, :]) + _mm(mla_ref[...], wo_ref[NA_WIDTH:, :])
    h1 = _layer_norm(DN_ALPHA * h_ref[...] + mix, g1_ref[...], b1_ref[...])
    h1b = h1.astype(BF16)
    h1b_ref[...] = h1b
    gate = jax.nn.sigmoid(_mm(h1b, wg_ref[...]) + bg_ref[...])
    ple = gate * _mm(p_ref[...].astype(BF16), wple_ref[...])
    r2_ref[...] = DN_ALPHA * h1 + ple
    q_t = _mm_nt(wqt_ref[...], h1b).astype(BF16)
    for blk in range(2 * PEER_HEADS):
        rows = slice(blk * PEER_HALF, (blk + 1) * PEER_HALF)
        sc_ref[rows, :] = _mm(keys_ref[blk % 2], q_t[rows, :])


def _mix(h, a_na, a_mla, p2, w_o, g1, b1, w_g, b_g, w_ple, w_qt, keys, tm):
    t = h.shape[0]
    row = lambda w: pl.BlockSpec((tm, w), lambda i: (i, 0))
    nq = 2 * PEER_HEADS * PEER_HALF
    return pl.pallas_call(
        _mix_body,
        grid=(t // tm,),
        in_specs=[row(D_MODEL), row(NA_WIDTH), row(MLA_HEADS * MLA_V), row(PLE_DIM), _full((D_MODEL, D_MODEL)),
                  _full((1, D_MODEL)), _full((1, D_MODEL)), _full((D_MODEL, D_MODEL)), _full((1, D_MODEL)),
                  _full((PLE_DIM, D_MODEL)), _full((nq, D_MODEL)), _full((2, PEER_KEYS, PEER_HALF))],
        out_specs=[row(D_MODEL), row(D_MODEL), pl.BlockSpec((nq, tm), lambda i: (0, i))],
        out_shape=[jax.ShapeDtypeStruct((t, D_MODEL), BF16), jax.ShapeDtypeStruct((t, D_MODEL), F32),
                   jax.ShapeDtypeStruct((nq, t), F32)],
        compiler_params=_params("parallel"),
        name="mix",
    )(h, a_na, a_mla, p2, w_o, g1, b1, w_g, b_g, w_ple, w_qt, keys)


def _top16(s, tag):
    big = jnp.int32(1 << 20)
    vals, tags = [], []
    for _ in range(PEER_TOPK):
        m = jnp.max(s, axis=0, keepdims=True)
        sel = jnp.min(jnp.where(s == m, tag, big), axis=0, keepdims=True)
        vals.append(m)
        tags.append(sel)
        s = jnp.where(tag == sel, -jnp.inf, s)
    return jnp.concatenate(vals, axis=0), jnp.concatenate(tags, axis=0)


_HALF_RANKS = PEER_TOPK // 2


def _topk_body(sc_ref, e1_ref, e2_ref, g_ref):
    tm = sc_ref.shape[1]
    key_id = lax.broadcasted_iota(jnp.int32, (PEER_KEYS, tm), 0)
    r16 = lax.broadcasted_iota(jnp.int32, (PEER_TOPK, tm), 0)
    r8 = lax.broadcasted_iota(jnp.int32, (_HALF_RANKS, tm), 0)
    flat = jnp.concatenate([r16] + [i * PEER_TOPK + r8 for i in range(1, _HALF_RANKS)]
                           + [(r8 + _HALF_RANKS) * PEER_TOPK], axis=0)
    e1s, e2s, gs = [], [], []
    for hh in range(PEER_HEADS):
        base = hh * 2 * PEER_HALF
        v1, i1 = _top16(sc_ref[base:base + PEER_HALF, :], key_id)
        v2, i2 = _top16(sc_ref[base + PEER_HALF:base + 2 * PEER_HALF, :], key_id)
        cand = jnp.concatenate(
            [v1[0:1, :] + v2] + [v1[i:i + 1, :] + v2[:_HALF_RANKS, :] for i in range(1, _HALF_RANKS)]
            + [v1[_HALF_RANKS:, :] + v2[0:1, :]], axis=0)
        cv, ci = _top16(cand, flat)
        ci1 = lax.shift_right_logical(ci, 4)
        ci2 = lax.bitwise_and(ci, PEER_TOPK - 1)
        e1 = jnp.zeros((PEER_TOPK, tm), jnp.int32)
        e2 = jnp.zeros((PEER_TOPK, tm), jnp.int32)
        for r in range(PEER_TOPK):
            e1 = jnp.where(ci1 == r, i1[r:r + 1, :], e1)
            e2 = jnp.where(ci2 == r, i2[r:r + 1, :], e2)
        p = jnp.exp(cv - jnp.max(cv, axis=0, keepdims=True))
        gs.append(p / jnp.sum(p, axis=0, keepdims=True))
        e1s.append(e1)
        e2s.append(e2)
    e1_ref[...] = jnp.concatenate(e1s, axis=0).T
    e2_ref[...] = jnp.concatenate(e2s, axis=0).T
    g_ref[...] = jnp.concatenate(gs, axis=0).T


def _topk(scores_t, tm):
    nq, t = scores_t.shape
    npair = PEER_HEADS * PEER_TOPK
    out = pl.BlockSpec((tm, npair), lambda i: (i, 0))
    return pl.pallas_call(
        _topk_body,
        grid=(t // tm,),
        in_specs=[pl.BlockSpec((nq, tm), lambda i: (0, i))],
        out_specs=[out, out, out],
        out_shape=[jax.ShapeDtypeStruct((t, npair), jnp.int32), jax.ShapeDtypeStruct((t, npair), jnp.int32),
                   jax.ShapeDtypeStruct((t, npair), F32)],
        compiler_params=_params("parallel"),
        name="topk",
    )(scores_t)


PEER_CHUNK = 32
WALL_PITCH = PEER_KEYS + 8


def _peer_body(x_ref, e1_ref, e2_ref, g_ref, u_ref, v_ref, r2_ref, g2_ref, b2_ref, o_ref, wall_ref, acc_ref):
    j = pl.program_id(1)
    tm = x_ref.shape[0]
    eb = u_ref.shape[0]
    npair = e1_ref.shape[1]

    @pl.when(j == 0)
    def _():
        acc_ref[...] = jnp.zeros_like(acc_ref)
        key_id = lax.broadcasted_iota(jnp.int32, (PEER_CHUNK, PEER_KEYS, 2 * npair), 1)

        def chunk(c, carry):
            rows = pl.ds(pl.multiple_of(c * PEER_CHUNK, PEER_CHUNK), PEER_CHUNK)
            e1 = e1_ref[rows, :]
            e2 = e2_ref[rows, :]
            g = g_ref[rows, :]
            g_hi = g.astype(BF16).astype(F32)
            gx = jnp.concatenate([g_hi, g - g_hi], axis=1)[:, None, :]
            e1x = jnp.concatenate([e1, e1], axis=1)[:, None, :]
            e2x = jnp.concatenate([e2, e2], axis=1)[:, None, :]
            a = jnp.where(e1x == key_id, 1.0, 0.0).astype(BF16)
            bm = jnp.where(e2x == key_id, gx, 0.0).astype(BF16)
            w = jnp.einsum("tep,tfp->tef", a, bm, preferred_element_type=F32)
            for i in range(PEER_CHUNK):
                base = pl.multiple_of((c * PEER_CHUNK + i) * WALL_PITCH, 8)
                wall_ref[pl.ds(base, PEER_KEYS), :] = w[i]
            return carry

        lax.fori_loop(0, tm // PEER_CHUNK, chunk, 0)

    nsub = eb // PEER_KEYS
    hpre = _mm_nt(x_ref[...], u_ref[...])
    w = jnp.concatenate([wall_ref[pl.ds(j * nsub + c, tm, stride=WALL_PITCH), :] for c in range(nsub)], axis=1)
    act = w * (0.5 * hpre * (1.0 + lax.erf(hpre * (2.0 ** -0.5))))
    acc_ref[...] += _mm(act.astype(BF16), v_ref[...])

    @pl.when(j == pl.num_programs(1) - 1)
    def _():
        o_ref[...] = _layer_norm(r2_ref[...] + acc_ref[...], g2_ref[...], b2_ref[...])


def _peer(h1b, e1, e2, g, u, v, r2, g2, b2, tm, eb):
    t = h1b.shape[0]
    n_exp = u.shape[0]
    npair = e1.shape[1]
    row = lambda w: pl.BlockSpec((tm, w), lambda i, j: (i, 0))
    tab = pl.BlockSpec((eb, D_MODEL), lambda i, j: (j, 0))
    vec = pl.BlockSpec((1, D_MODEL), lambda i, j: (0, 0))
    return pl.pallas_call(
        _peer_body,
        grid=(t // tm, n_exp // eb),
        in_specs=[row(D_MODEL), row(npair), row(npair), row(npair), tab, tab, row(D_MODEL), vec, vec],
        out_specs=row(D_MODEL),
        out_shape=jax.ShapeDtypeStruct((t, D_MODEL), F32),
        scratch_shapes=[pltpu.VMEM((tm * WALL_PITCH, PEER_KEYS), F32), pltpu.VMEM((tm, D_MODEL), F32)],
        compiler_params=_params("parallel", "arbitrary"),
        name="peer",
    )(h1b, e1, e2, g, u, v, r2, g2, b2)


def _rope_tables(seq):
    t = jnp.arange(seq)
    row = (t // GRID_W).astype(F32)
    col = (t % GRID_W).astype(F32)
    axis_dim = MLA_ROPE // 2
    inv = ROPE_BASE ** (-jnp.arange(0, axis_dim, 2, dtype=F32) / axis_dim)
    ang = jnp.concatenate([row[:, None] * inv[None, :], col[:, None] * inv[None, :]], axis=-1)
    cos, sin = jnp.cos(ang), jnp.sin(ang)
    pad = LANES - MLA_NOPE - MLA_ROPE
    cos128 = jnp.concatenate([jnp.ones((seq, MLA_NOPE), F32), cos, cos, jnp.zeros((seq, pad), F32)], axis=1)
    sin128 = jnp.concatenate([jnp.zeros((seq, MLA_NOPE), F32), -sin, sin, jnp.zeros((seq, pad), F32)], axis=1)
    return cos128, sin128


_PERM = np.concatenate([np.arange(0, MLA_ROPE, 2), np.arange(1, MLA_ROPE, 2)])
_PERM_SW = np.concatenate([np.arange(1, MLA_ROPE, 2), np.arange(0, MLA_ROPE, 2)])


def _prep_weights(w_in, w_uq, w_ukv):
    pad = LANES - MLA_NOPE - MLA_ROPE
    zin = lambda n: jnp.zeros((D_MODEL, n), F32)
    kr = w_in[:, C_CKV:]
    w_in_ext = jnp.concatenate(
        [w_in[:, :C_CKV], zin(MLA_NOPE), kr[:, _PERM], zin(pad), zin(MLA_NOPE), kr[:, _PERM_SW], zin(pad)], axis=1)
    wq = w_uq.reshape(MLA_Q_RANK, MLA_HEADS, MLA_QD)
    zq = lambda n: jnp.zeros((MLA_Q_RANK, MLA_HEADS, n), F32)
    rope = wq[:, :, MLA_NOPE:]
    wq_main = jnp.concatenate([wq[:, :, :MLA_NOPE], rope[:, :, _PERM], zq(pad)], axis=2)
    wq_sw = jnp.concatenate([zq(MLA_NOPE), rope[:, :, _PERM_SW], zq(pad)], axis=2)
    wkv = w_ukv.reshape(MLA_KV_RANK, MLA_HEADS, MLA_NOPE + MLA_V)
    zk = jnp.zeros((MLA_KV_RANK, MLA_HEADS, LANES - MLA_NOPE), F32)
    wk_pad = jnp.concatenate([wkv[:, :, :MLA_NOPE], zk], axis=2)
    vv = wkv[:, :, MLA_NOPE:]
    zv = jnp.zeros_like(vv)
    odd = (jnp.arange(MLA_HEADS) % 2 == 1)[None, :, None]
    wv_pad = jnp.where(odd, jnp.concatenate([zv, vv], axis=2), jnp.concatenate([vv, zv], axis=2))
    hp = MLA_HEADS * LANES
    flat = lambda w: w.reshape(w.shape[0], hp).astype(BF16)
    return w_in_ext.astype(BF16), flat(wq_main), flat(wq_sw), flat(wk_pad), flat(wv_pad)


def kernel(x, p, emb_ln_g, emb_ln_b, w_in, mla_q_norm_g, mla_kv_norm_g, w_uq, w_ukv, na_rpb, w_o, ln1_g, ln1_b,
           peer_w_q, peer_sub_keys, peer_u, peer_v, ple_w, ple_gate_w, ple_gate_b, ln2_g, ln2_b):
    b, seq, d = x.shape
    assert d == D_MODEL and seq % GRID_W == 0 and w_in.shape[0] == DEPTH
    t = b * seq
    vec = lambda a: a.reshape(1, -1).astype(F32)
    w_in_ext, wq_main, wq_sw, wk_pad, wv_pad = _prep_weights(w_in[0], w_uq[0], w_ukv[0])
    cos128, sin128 = _rope_tables(seq)

    h, qkv_na, q_mla, k_mla, v_mla = _proj(
        x.reshape(t, d), vec(emb_ln_g), vec(emb_ln_b), w_in_ext, vec(mla_q_norm_g[0]), vec(mla_kv_norm_g[0]),
        wq_main, wq_sw, wk_pad, wv_pad, cos128, sin128, seq, tm=256)

    bias = _na_bias(na_rpb[0].reshape(-1).astype(F32))
    a_na = _na_attn(qkv_na.reshape(b, seq, C_NA), bias, seq)
    hp = MLA_HEADS * LANES
    a_mla = _mla_attn(q_mla.reshape(b, seq, hp), k_mla.reshape(b, seq, hp), v_mla.reshape(b, seq, hp), tq=512)

    h1b, r2, scores_t = _mix(
        h, a_na.reshape(t, NA_WIDTH), a_mla.reshape(t, MLA_HEADS * MLA_V), p[0].reshape(t, PLE_DIM),
        w_o[0].astype(BF16), vec(ln1_g[0]), vec(ln1_b[0]), ple_gate_w[0].astype(BF16), vec(ple_gate_b[0]),
        ple_w[0].astype(BF16), peer_w_q[0].T.astype(BF16), peer_sub_keys[0].astype(BF16), tm=256)

    e1, e2, gates = _topk(scores_t, tm=256)
    out = _peer(h1b, e1, e2, gates, peer_u[0].astype(BF16), peer_v[0].astype(BF16), r2,
                vec(ln2_g[0]), vec(ln2_b[0]), tm=256, eb=2048)
    return out.reshape(b, seq, d)
```

```python
import functools

import numpy as np
import jax
import jax.numpy as jnp
from jax import lax
from jax.experimental import pallas as pl
from jax.experimental.pallas import tpu as pltpu

F32 = jnp.float32
BF16 = jnp.bfloat16

D_MODEL = 1024
GRID_W = 64
NA_HEADS = 8
NA_HEAD_DIM = 64
NA_WIN_H = 8
NA_WIN_W = 16
NA_WIDTH = NA_HEADS * NA_HEAD_DIM
NA_BIAS_ROWS = 2 * NA_WIN_H - 1
NA_BIAS_COLS = 2 * NA_WIN_W - 1
MLA_HEADS = 8
MLA_Q_RANK = 384
MLA_KV_RANK = 256
MLA_NOPE = 64
MLA_ROPE = 32
MLA_V = 64
MLA_QD = MLA_NOPE + MLA_ROPE
ROPE_BASE = 10000.0
PEER_HEADS = 8
PEER_KEYS = 128
PEER_HALF = 128
PEER_TOPK = 16
PLE_DIM = 256
DEPTH = 1
DN_ALPHA = float((2 * DEPTH) ** 0.25)
LN_EPS = 1e-5
LANES = 128
NEG = -1e30
VMEM_LIMIT = 56 * 1024 * 1024

C_NA = 3 * NA_WIDTH
C_CQ = C_NA + MLA_Q_RANK
C_CKV = C_CQ + MLA_KV_RANK
C_KRA = C_CKV + LANES
C_IN = C_KRA + LANES

NT = (((1,), (1,)), ((), ()))


def _layer_norm(x, g, b):
    mu = jnp.mean(x, axis=-1, keepdims=True)
    xc = x - mu
    var = jnp.mean(xc * xc, axis=-1, keepdims=True)
    return xc * lax.rsqrt(var + LN_EPS) * g + b


def _rms_norm(x, g):
    return x * lax.rsqrt(jnp.mean(x * x, axis=-1, keepdims=True) + LN_EPS) * g


def _mm(a, b):
    return jnp.dot(a, b, preferred_element_type=F32)


def _mm_nt(a, b):
    return lax.dot_general(a, b, NT, preferred_element_type=F32)


def _params(*sem):
    return pltpu.CompilerParams(dimension_semantics=sem, vmem_limit_bytes=VMEM_LIMIT)


def _full(shape):
    return pl.BlockSpec(shape, lambda *_: (0,) * len(shape))


def _proj_body(x_ref, g0_ref, b0_ref, win_ref, qg_ref, kvg_ref, wqm_ref, wqs_ref, wk_ref, wv_ref,
               cos_ref, sin_ref, h_ref, na_ref, q_ref, k_ref, v_ref):
    h = _layer_norm(x_ref[...], g0_ref[...], b0_ref[...])
    h_ref[...] = h
    z = _mm(h.astype(BF16), win_ref[...])
    na_ref[:, :NA_WIDTH] = (z[:, :NA_WIDTH] * (NA_HEAD_DIM ** -0.5)).astype(BF16)
    na_ref[:, NA_WIDTH:] = z[:, NA_WIDTH:C_NA].astype(BF16)
    cqn = _rms_norm(z[:, C_NA:C_CQ], qg_ref[...]).astype(BF16)
    ckvn = _rms_norm(z[:, C_CQ:C_CKV], kvg_ref[...]).astype(BF16)
    cos = cos_ref[...]
    sin = sin_ref[...]
    q = _mm(cqn, wqm_ref[...]) * jnp.tile(cos, (1, MLA_HEADS)) + _mm(cqn, wqs_ref[...]) * jnp.tile(sin, (1, MLA_HEADS))
    q_ref[...] = (q * (MLA_QD ** -0.5)).astype(BF16)
    k_rot = z[:, C_CKV:C_KRA] * cos + z[:, C_KRA:C_IN] * sin
    k_ref[...] = (_mm(ckvn, wk_ref[...]) + jnp.tile(k_rot, (1, MLA_HEADS))).astype(BF16)
    v_ref[...] = _mm(ckvn, wv_ref[...]).astype(BF16)


def _proj(x2, g0, b0, w_in_ext, qg, kvg, wq_main, wq_sw, wk_pad, wv_pad, cos128, sin128, seq, tm):
    t = x2.shape[0]
    n_pos = seq // tm
    row = lambda w: pl.BlockSpec((tm, w), lambda i: (i, 0))
    pos = pl.BlockSpec((tm, LANES), lambda i: (i % n_pos, 0))
    hp = MLA_HEADS * LANES
    return pl.pallas_call(
        _proj_body,
        grid=(t // tm,),
        in_specs=[row(D_MODEL), _full((1, D_MODEL)), _full((1, D_MODEL)), _full((D_MODEL, C_IN)),
                  _full((1, MLA_Q_RANK)), _full((1, MLA_KV_RANK)), _full((MLA_Q_RANK, hp)), _full((MLA_Q_RANK, hp)),
                  _full((MLA_KV_RANK, hp)), _full((MLA_KV_RANK, hp)), pos, pos],
        out_specs=[row(D_MODEL), row(C_NA), row(hp), row(hp), row(hp)],
        out_shape=[jax.ShapeDtypeStruct((t, D_MODEL), F32), jax.ShapeDtypeStruct((t, C_NA), BF16),
                   jax.ShapeDtypeStruct((t, hp), BF16), jax.ShapeDtypeStruct((t, hp), BF16),
                   jax.ShapeDtypeStruct((t, hp), BF16)],
        compiler_params=_params("parallel"),
        name="proj",
    )(x2, g0, b0, w_in_ext, qg, kvg, wq_main, wq_sw, wk_pad, wv_pad, cos128, sin128)


def _na_bias_body(rpb_ref, o_ref):
    hh = pl.program_id(0)
    lane = lax.broadcasted_iota(jnp.int32, (GRID_W, LANES), 1)
    qc = lax.broadcasted_iota(jnp.int32, (GRID_W, LANES), 0)
    kc = lane % GRID_W
    first = lane < GRID_W
    dj = kc - qc + (NA_WIN_W - 1)
    cs = jnp.clip(qc - NA_WIN_W // 2, 0, GRID_W - NA_WIN_W)
    valid = (kc >= cs) & (kc < cs + NA_WIN_W)
    pair = []
    for a in range(NA_BIAS_ROWS - 1):
        acc = jnp.full((GRID_W, LANES), NEG, F32)
        for d in range(NA_BIAS_COLS):
            lo = rpb_ref[(hh * NA_BIAS_ROWS + a) * NA_BIAS_COLS + d]
            hi = rpb_ref[(hh * NA_BIAS_ROWS + a + 1) * NA_BIAS_COLS + d]
            acc = jnp.where(valid & (dj == d), jnp.where(first, lo, hi), acc)
        pair.append(acc)
    for d0 in range(NA_WIN_H):
        o_ref[0, d0] = jnp.concatenate([pair[d0 + 2 * i] for i in range(NA_WIN_H // 2)], axis=1)


def _na_bias(rpb_flat):
    band = NA_WIN_H * GRID_W
    return pl.pallas_call(
        _na_bias_body,
        grid=(NA_HEADS,),
        in_specs=[pl.BlockSpec(memory_space=pltpu.SMEM)],
        out_specs=pl.BlockSpec((1, NA_WIN_H, GRID_W, band), lambda h: (h, 0, 0, 0)),
        out_shape=jax.ShapeDtypeStruct((NA_HEADS, NA_WIN_H, GRID_W, band), F32),
        compiler_params=_params("arbitrary"),
        name="na_bias",
    )(rpb_flat)


NA_ROW_UNROLL = 2


def _na_body(q_ref, k_ref, v_ref, bias_ref, o_ref, *, rows):
    first = lax.broadcasted_iota(jnp.int32, (GRID_W, LANES), 1) < NA_HEAD_DIM
    band = NA_WIN_H * GRID_W

    def one_row(r, carry):
        rs = jnp.clip(r - NA_WIN_H // 2, 0, rows - NA_WIN_H)
        d0 = rs - r + (NA_WIN_H - 1)
        q0 = pl.multiple_of(r * GRID_W, GRID_W)
        k0 = pl.multiple_of(rs * GRID_W, GRID_W)
        pairs = range(NA_HEADS // 2)
        scores = []
        for j in pairs:
            cols = slice(j * LANES, (j + 1) * LANES)
            qp = q_ref[0, pl.ds(q0, GRID_W), cols]
            zero = jnp.zeros_like(qp)
            qs = jnp.concatenate([jnp.where(first, qp, zero), jnp.where(first, zero, qp)], axis=0)
            bias = jnp.concatenate([bias_ref[2 * j, d0], bias_ref[2 * j + 1, d0]], axis=0)
            scores.append(_mm_nt(qs, k_ref[0, pl.ds(k0, band), cols]) + bias)
        probs, sums = [], []
        for s in scores:
            p = jnp.exp(s - jnp.max(s, axis=-1, keepdims=True))
            sums.append(jnp.sum(p, axis=-1, keepdims=True))
            probs.append(p.astype(BF16))
        for j in pairs:
            cols = slice(j * LANES, (j + 1) * LANES)
            o = _mm(probs[j], v_ref[0, pl.ds(k0, band), cols]) / sums[j]
            o_ref[0, pl.ds(q0, GRID_W), cols] = jnp.where(first, o[:GRID_W], o[GRID_W:]).astype(BF16)
        return carry

    lax.fori_loop(0, rows, one_row, 0, unroll=NA_ROW_UNROLL)


def _na_attn(qkv, bias, seq):
    b = qkv.shape[0]
    rows = seq // GRID_W
    part = lambda c: pl.BlockSpec((1, seq, NA_WIDTH), lambda i: (i, 0, c))
    return pl.pallas_call(
        functools.partial(_na_body, rows=rows),
        grid=(b,),
        in_specs=[part(0), part(1), part(2), _full(bias.shape)],
        out_specs=pl.BlockSpec((1, seq, NA_WIDTH), lambda i: (i, 0, 0)),
        out_shape=jax.ShapeDtypeStruct((b, seq, NA_WIDTH), BF16),
        compiler_params=_params("parallel"),
        name="na_attn",
    )(qkv, qkv, qkv, bias)


def _mla_body(q_ref, k_ref, v_ref, o_ref):
    for j in range(MLA_HEADS // 2):
        acc = None
        for e in range(2):
            cols = slice((2 * j + e) * LANES, (2 * j + e + 1) * LANES)
            s = _mm_nt(q_ref[0, :, cols], k_ref[0, :, cols])
            m = jnp.max(s, axis=-1, keepdims=True)
            p = jnp.exp(s - m)
            l = jnp.sum(p, axis=-1, keepdims=True)
            o = _mm(p.astype(BF16), v_ref[0, :, cols]) / l
            acc = o if acc is None else acc + o
        o_ref[0, :, j * LANES:(j + 1) * LANES] = acc.astype(BF16)


def _mla_attn(q, k, v, tq):
    b, seq, hp = q.shape
    kv = pl.BlockSpec((1, seq, hp), lambda i, j: (i, 0, 0))
    width = MLA_HEADS * MLA_V
    return pl.pallas_call(
        _mla_body,
        grid=(b, seq // tq),
        in_specs=[pl.BlockSpec((1, tq, hp), lambda i, j: (i, j, 0)), kv, kv],
        out_specs=pl.BlockSpec((1, tq, width), lambda i, j: (i, j, 0)),
        out_shape=jax.ShapeDtypeStruct((b, seq, width), BF16),
        compiler_params=_params("parallel", "arbitrary"),
        name="mla_attn",
    )(q, k, v)


def _mix_body(h_ref, na_ref, mla_ref, p_ref, wo_ref, g1_ref, b1_ref, wg_ref, bg_ref, wple_ref, wqt_ref, keys_ref,
              h1b_ref, r2_ref, sc_ref):
    mix = _mm(na_ref[...], wo_ref[:NA_WIDTH, :]) + _mm(mla_ref[...], wo_ref[NA_WIDTH:, :])
    h1 = _layer_norm(DN_ALPHA * h_ref[...] + mix, g1_ref[...], b1_ref[...])
    h1b = h1.astype(BF16)
    h1b_ref[...] = h1b
    gate = jax.nn.sigmoid(_mm(h1b, wg_ref[...]) + bg_ref[...])
    ple = gate * _mm(p_ref[...].astype(BF16), wple_ref[...])
    r2_ref[...] = DN_ALPHA * h1 + ple
    q_t = _mm_nt(wqt_ref[...], h1b).astype(BF16)
    for blk in range(2 * PEER_HEADS):
        rows = slice(blk * PEER_HALF, (blk + 1) * PEER_HALF)
        sc_ref[rows, :] = _mm(keys_ref[blk % 2], q_t[rows, :])


def _mix(h, a_na, a_mla, p2, w_o, g1, b1, w_g, b_g, w_ple, w_qt, keys, tm):
    t = h.shape[0]
    row = lambda w: pl.BlockSpec((tm, w), lambda i: (i, 0))
    nq = 2 * PEER_HEADS * PEER_HALF
    return pl.pallas_call(
        _mix_body,
        grid=(t // tm,),
        in_specs=[row(D_MODEL), row(NA_WIDTH), row(MLA_HEADS * MLA_V), row(PLE_DIM), _full((D_MODEL, D_MODEL)),
                  _full((1, D_MODEL)), _full((1, D_MODEL)), _full((D_MODEL, D_MODEL)), _full((1, D_MODEL)),
                  _full((PLE_DIM, D_MODEL)), _full((nq, D_MODEL)), _full((2, PEER_KEYS, PEER_HALF))],
        out_specs=[row(D_MODEL), row(D_MODEL), pl.BlockSpec((nq, tm), lambda i: (0, i))],
        out_shape=[jax.ShapeDtypeStruct((t, D_MODEL), BF16), jax.ShapeDtypeStruct((t, D_MODEL), F32),
                   jax.ShapeDtypeStruct((nq, t), F32)],
        compiler_params=_params("parallel"),
        name="mix",
    )(h, a_na, a_mla, p2, w_o, g1, b1, w_g, b_g, w_ple, w_qt, keys)


def _top16(s, tag):
    big = jnp.int32(1 << 20)
    vals, tags = [], []
    for _ in range(PEER_TOPK):
        m = jnp.max(s, axis=0, keepdims=True)
        sel = jnp.min(jnp.where(s == m, tag, big), axis=0, keepdims=True)
        vals.append(m)
        tags.append(sel)
        s = jnp.where(tag == sel, -jnp.inf, s)
    return jnp.concatenate(vals, axis=0), jnp.concatenate(tags, axis=0)


_HALF_RANKS = PEER_TOPK // 2


def _topk_body(sc_ref, e1_ref, e2_ref, g_ref):
    tm = sc_ref.shape[1]
    key_id = lax.broadcasted_iota(jnp.int32, (PEER_KEYS, tm), 0)
    r16 = lax.broadcasted_iota(jnp.int32, (PEER_TOPK, tm), 0)
    r8 = lax.broadcasted_iota(jnp.int32, (_HALF_RANKS, tm), 0)
    flat = jnp.concatenate([r16] + [i * PEER_TOPK + r8 for i in range(1, _HALF_RANKS)]
                           + [(r8 + _HALF_RANKS) * PEER_TOPK], axis=0)
    e1s, e2s, gs = [], [], []
    for hh in range(PEER_HEADS):
        base = hh * 2 * PEER_HALF
        v1, i1 = _top16(sc_ref[base:base + PEER_HALF, :], key_id)
        v2, i2 = _top16(sc_ref[base + PEER_HALF:base + 2 * PEER_HALF, :], key_id)
        cand = jnp.concatenate(
            [v1[0:1, :] + v2] + [v1[i:i + 1, :] + v2[:_HALF_RANKS, :] for i in range(1, _HALF_RANKS)]
            + [v1[_HALF_RANKS:, :] + v2[0:1, :]], axis=0)
        cv, ci = _top16(cand, flat)
        ci1 = lax.shift_right_logical(ci, 4)
        ci2 = lax.bitwise_and(ci, PEER_TOPK - 1)
        e1 = jnp.zeros((PEER_TOPK, tm), jnp.int32)
        e2 = jnp.zeros((PEER_TOPK, tm), jnp.int32)
        for r in range(PEER_TOPK):
            e1 = jnp.where(ci1 == r, i1[r:r + 1, :], e1)
            e2 = jnp.where(ci2 == r, i2[r:r + 1, :], e2)
        p = jnp.exp(cv - jnp.max(cv, axis=0, keepdims=True))
        gs.append(p / jnp.sum(p, axis=0, keepdims=True))
        e1s.append(e1)
        e2s.append(e2)
    e1_ref[...] = jnp.concatenate(e1s, axis=0).T
    e2_ref[...] = jnp.concatenate(e2s, axis=0).T
    g_ref[...] = jnp.concatenate(gs, axis=0).T


def _topk(scores_t, tm):
    nq, t = scores_t.shape
    npair = PEER_HEADS * PEER_TOPK
    out = pl.BlockSpec((tm, npair), lambda i: (i, 0))
    return pl.pallas_call(
        _topk_body,
        grid=(t // tm,),
        in_specs=[pl.BlockSpec((nq, tm), lambda i: (0, i))],
        out_specs=[out, out, out],
        out_shape=[jax.ShapeDtypeStruct((t, npair), jnp.int32), jax.ShapeDtypeStruct((t, npair), jnp.int32),
                   jax.ShapeDtypeStruct((t, npair), F32)],
        compiler_params=_params("parallel"),
        name="topk",
    )(scores_t)


PEER_CHUNK = 32
WALL_ROWS = PEER_KEYS // 2
WALL_PITCH = WALL_ROWS + 8
HIGH_HALF = -65536


def _peer_body(x_ref, e1_ref, e2_ref, g_ref, u_ref, v_ref, r2_ref, g2_ref, b2_ref, o_ref, wall_ref, acc_ref):
    j = pl.program_id(1)
    tm = x_ref.shape[0]
    eb = u_ref.shape[0]
    npair = e1_ref.shape[1]

    @pl.when(j == 0)
    def _():
        acc_ref[...] = jnp.zeros_like(acc_ref)
        m = lax.broadcasted_iota(jnp.int32, (PEER_CHUNK, PEER_KEYS, npair), 1)
        key1 = jnp.where(m < WALL_ROWS, 2 * m, 2 * (m - WALL_ROWS) + 1)

        def chunk(c, carry):
            rows = pl.ds(pl.multiple_of(c * PEER_CHUNK, PEER_CHUNK), PEER_CHUNK)
            a = jnp.where(e1_ref[rows, :][:, None, :] == key1, 1.0, 0.0).astype(BF16)
            bm = jnp.where(e2_ref[rows, :][:, None, :] == m, g_ref[rows, :][:, None, :], 0.0).astype(BF16)
            w = jnp.einsum("tep,tfp->tef", a, bm, preferred_element_type=F32)
            as_word = lambda z: pltpu.bitcast(z.astype(BF16).astype(F32), jnp.int32)
            word = as_word(w[:, WALL_ROWS:]) | lax.shift_right_logical(as_word(w[:, :WALL_ROWS]), 16)
            for i in range(PEER_CHUNK):
                base = pl.multiple_of((c * PEER_CHUNK + i) * WALL_PITCH, 8)
                wall_ref[pl.ds(base, WALL_ROWS), :] = word[i]
            return carry

        lax.fori_loop(0, tm // PEER_CHUNK, chunk, 0)

    nword = eb // PEER_KEYS // 2
    hpre = _mm_nt(x_ref[...], u_ref[...])
    parts = []
    for c in range(nword):
        word = wall_ref[pl.ds(j * nword + c, tm, stride=WALL_PITCH), :]
        parts.append(pltpu.bitcast(lax.shift_left(word, 16), F32))
        parts.append(pltpu.bitcast(word & HIGH_HALF, F32))
    w = jnp.concatenate(parts, axis=1)
    act = w * (0.5 * hpre * (1.0 + lax.erf(hpre * (2.0 ** -0.5))))
    acc_ref[...] += _mm(act.astype(BF16), v_ref[...])

    @pl.when(j == pl.num_programs(1) - 1)
    def _():
        o_ref[...] = _layer_norm(r2_ref[...] + acc_ref[...], g2_ref[...], b2_ref[...])


def _peer(h1b, e1, e2, g, u, v, r2, g2, b2, tm, eb):
    t = h1b.shape[0]
    n_exp = u.shape[0]
    npair = e1.shape[1]
    row = lambda w: pl.BlockSpec((tm, w), lambda i, j: (i, 0))
    tab = pl.BlockSpec((eb, D_MODEL), lambda i, j: (j, 0))
    vec = pl.BlockSpec((1, D_MODEL), lambda i, j: (0, 0))
    return pl.pallas_call(
        _peer_body,
        grid=(t // tm, n_exp // eb),
        in_specs=[row(D_MODEL), row(npair), row(npair), row(npair), tab, tab, row(D_MODEL), vec, vec],
        out_specs=row(D_MODEL),
        out_shape=jax.ShapeDtypeStruct((t, D_MODEL), F32),
        scratch_shapes=[pltpu.VMEM((tm * WALL_PITCH, PEER_KEYS), jnp.int32), pltpu.VMEM((tm, D_MODEL), F32)],
        compiler_params=_params("parallel", "arbitrary"),
        name="peer",
    )(h1b, e1, e2, g, u, v, r2, g2, b2)


def _rope_tables(seq):
    t = jnp.arange(seq)
    row = (t // GRID_W).astype(F32)
    col = (t % GRID_W).astype(F32)
    axis_dim = MLA_ROPE // 2
    inv = ROPE_BASE ** (-jnp.arange(0, axis_dim, 2, dtype=F32) / axis_dim)
    ang = jnp.concatenate([row[:, None] * inv[None, :], col[:, None] * inv[None, :]], axis=-1)
    cos, sin = jnp.cos(ang), jnp.sin(ang)
    pad = LANES - MLA_NOPE - MLA_ROPE
    cos128 = jnp.concatenate([jnp.ones((seq, MLA_NOPE), F32), cos, cos, jnp.zeros((seq, pad), F32)], axis=1)
    sin128 = jnp.concatenate([jnp.zeros((seq, MLA_NOPE), F32), -sin, sin, jnp.zeros((seq, pad), F32)], axis=1)
    return cos128, sin128


_PERM = np.concatenate([np.arange(0, MLA_ROPE, 2), np.arange(1, MLA_ROPE, 2)])
_PERM_SW = np.concatenate([np.arange(1, MLA_ROPE, 2), np.arange(0, MLA_ROPE, 2)])


def _prep_weights(w_in, w_uq, w_ukv):
    pad = LANES - MLA_NOPE - MLA_ROPE
    zin = lambda n: jnp.zeros((D_MODEL, n), F32)
    kr = w_in[:, C_CKV:]
    w_in_ext = jnp.concatenate(
        [w_in[:, :C_CKV], zin(MLA_NOPE), kr[:, _PERM], zin(pad), zin(MLA_NOPE), kr[:, _PERM_SW], zin(pad)], axis=1)
    wq = w_uq.reshape(MLA_Q_RANK, MLA_HEADS, MLA_QD)
    zq = lambda n: jnp.zeros((MLA_Q_RANK, MLA_HEADS, n), F32)
    rope = wq[:, :, MLA_NOPE:]
    wq_main = jnp.concatenate([wq[:, :, :MLA_NOPE], rope[:, :, _PERM], zq(pad)], axis=2)
    wq_sw = jnp.concatenate([zq(MLA_NOPE), rope[:, :, _PERM_SW], zq(pad)], axis=2)
    wkv = w_ukv.reshape(MLA_KV_RANK, MLA_HEADS, MLA_NOPE + MLA_V)
    zk = jnp.zeros((MLA_KV_RANK, MLA_HEADS, LANES - MLA_NOPE), F32)
    wk_pad = jnp.concatenate([wkv[:, :, :MLA_NOPE], zk], axis=2)
    vv = wkv[:, :, MLA_NOPE:]
    zv = jnp.zeros_like(vv)
    odd = (jnp.arange(MLA_HEADS) % 2 == 1)[None, :, None]
    wv_pad = jnp.where(odd, jnp.concatenate([zv, vv], axis=2), jnp.concatenate([vv, zv], axis=2))
    hp = MLA_HEADS * LANES
    flat = lambda w: w.reshape(w.shape[0], hp).astype(BF16)
    return w_in_ext.astype(BF16), flat(wq_main), flat(wq_sw), flat(wk_pad), flat(wv_pad)


def kernel(x, p, emb_ln_g, emb_ln_b, w_in, mla_q_norm_g, mla_kv_norm_g, w_uq, w_ukv, na_rpb, w_o, ln1_g, ln1_b,
           peer_w_q, peer_sub_keys, peer_u, peer_v, ple_w, ple_gate_w, ple_gate_b, ln2_g, ln2_b):
    b, seq, d = x.shape
    assert d == D_MODEL and seq % GRID_W == 0 and w_in.shape[0] == DEPTH
    t = b * seq
    vec = lambda a: a.reshape(1, -1).astype(F32)
    w_in_ext, wq_main, wq_sw, wk_pad, wv_pad = _prep_weights(w_in[0], w_uq[0], w_ukv[0])
    cos128, sin128 = _rope_tables(seq)

    h, qkv_na, q_mla, k_mla, v_mla = _proj(
        x.reshape(t, d), vec(emb_ln_g), vec(emb_ln_b), w_in_ext, vec(mla_q_norm_g[0]), vec(mla_kv_norm_g[0]),
        wq_main, wq_sw, wk_pad, wv_pad, cos128, sin128, seq, tm=256)

    bias = _na_bias(na_rpb[0].reshape(-1).astype(F32))
    a_na = _na_attn(qkv_na.reshape(b, seq, C_NA), bias, seq)
    hp = MLA_HEADS * LANES
    a_mla = _mla_attn(q_mla.reshape(b, seq, hp), k_mla.reshape(b, seq, hp), v_mla.reshape(b, seq, hp), tq=512)

    h1b, r2, scores_t = _mix(
        h, a_na.reshape(t, NA_WIDTH), a_mla.reshape(t, MLA_HEADS * MLA_V), p[0].reshape(t, PLE_DIM),
        w_o[0].astype(BF16), vec(ln1_g[0]), vec(ln1_b[0]), ple_gate_w[0].astype(BF16), vec(ple_gate_b[0]),
        ple_w[0].astype(BF16), peer_w_q[0].T.astype(BF16), peer_sub_keys[0].astype(BF16), tm=256)

    e1, e2, gates = _topk(scores_t, tm=256)
    out = _peer(h1b, e1, e2, gates, peer_u[0].astype(BF16), peer_v[0].astype(BF16), r2,
                vec(ln2_g[0]), vec(ln2_b[0]), tm=512, eb=1024)
    return out.reshape(b, seq, d)
```

```python
import functools

import numpy as np
import jax
import jax.numpy as jnp
from jax import lax
from jax.experimental import pallas as pl
from jax.experimental.pallas import tpu as pltpu

F32 = jnp.float32
BF16 = jnp.bfloat16

D_MODEL = 1024
GRID_W = 64
NA_HEADS = 8
NA_HEAD_DIM = 64
NA_WIN_H = 8
NA_WIN_W = 16
NA_WIDTH = NA_HEADS * NA_HEAD_DIM
NA_BIAS_ROWS = 2 * NA_WIN_H - 1
NA_BIAS_COLS = 2 * NA_WIN_W - 1
MLA_HEADS = 8
MLA_Q_RANK = 384
MLA_KV_RANK = 256
MLA_NOPE = 64
MLA_ROPE = 32
MLA_V = 64
MLA_QD = MLA_NOPE + MLA_ROPE
ROPE_BASE = 10000.0
PEER_HEADS = 8
PEER_KEYS = 128
PEER_HALF = 128
PEER_TOPK = 16
PLE_DIM = 256
DEPTH = 1
DN_ALPHA = float((2 * DEPTH) ** 0.25)
LN_EPS = 1e-5
LANES = 128
NEG = -1e30
VMEM_LIMIT = 56 * 1024 * 1024

PROJ_TM = 512
MLA_TQ = 512
MIX_TM = 512
PEER_TM = 256
PEER_EB = 2048

C_NA = 3 * NA_WIDTH
C_CQ = C_NA + MLA_Q_RANK
C_CKV = C_CQ + MLA_KV_RANK
C_KRA = C_CKV + LANES
C_IN = C_KRA + LANES

NT = (((1,), (1,)), ((), ()))


def _layer_norm(x, g, b):
    mu = jnp.mean(x, axis=-1, keepdims=True)
    xc = x - mu
    var = jnp.mean(xc * xc, axis=-1, keepdims=True)
    return xc * lax.rsqrt(var + LN_EPS) * g + b


def _rms_norm(x, g):
    return x * lax.rsqrt(jnp.mean(x * x, axis=-1, keepdims=True) + LN_EPS) * g


def _mm(a, b):
    return jnp.dot(a, b, preferred_element_type=F32)


def _mm_nt(a, b):
    return lax.dot_general(a, b, NT, preferred_element_type=F32)


def _params(*sem):
    return pltpu.CompilerParams(dimension_semantics=sem, vmem_limit_bytes=VMEM_LIMIT)


def _full(shape):
    return pl.BlockSpec(shape, lambda *_: (0,) * len(shape))


def _proj_body(x_ref, g0_ref, b0_ref, win_ref, qg_ref, kvg_ref, wqm_ref, wqs_ref, wk_ref, wv_ref,
               cos_ref, sin_ref, h_ref, na_ref, q_ref, k_ref, vt_ref):
    h = _layer_norm(x_ref[...], g0_ref[...], b0_ref[...])
    h_ref[...] = h
    z = _mm(h.astype(BF16), win_ref[...])
    na_ref[:, :NA_WIDTH] = (z[:, :NA_WIDTH] * (NA_HEAD_DIM ** -0.5)).astype(BF16)
    na_ref[:, NA_WIDTH:] = z[:, NA_WIDTH:C_NA].astype(BF16)
    cqn = _rms_norm(z[:, C_NA:C_CQ], qg_ref[...]).astype(BF16)
    ckvn = _rms_norm(z[:, C_CQ:C_CKV], kvg_ref[...]).astype(BF16)
    cos = cos_ref[...]
    sin = sin_ref[...]
    q = _mm(cqn, wqm_ref[...]) * jnp.tile(cos, (1, MLA_HEADS)) + _mm(cqn, wqs_ref[...]) * jnp.tile(sin, (1, MLA_HEADS))
    q_ref[...] = (q * (MLA_QD ** -0.5)).astype(BF16)
    k_rot = z[:, C_CKV:C_KRA] * cos + z[:, C_KRA:C_IN] * sin
    k_ref[...] = (_mm(ckvn, wk_ref[...]) + jnp.tile(k_rot, (1, MLA_HEADS))).astype(BF16)
    vt_ref[...] = _mm_nt(wv_ref[...], ckvn).astype(BF16)


def _proj(x2, g0, b0, w_in_ext, qg, kvg, wq_main, wq_sw, wk_pad, wv_pad_t, cos128, sin128, seq, tm):
    t = x2.shape[0]
    n_pos = seq // tm
    row = lambda w: pl.BlockSpec((tm, w), lambda i: (i, 0))
    pos = pl.BlockSpec((tm, LANES), lambda i: (i % n_pos, 0))
    hp = MLA_HEADS * LANES
    return pl.pallas_call(
        _proj_body,
        grid=(t // tm,),
        in_specs=[row(D_MODEL), _full((1, D_MODEL)), _full((1, D_MODEL)), _full((D_MODEL, C_IN)),
                  _full((1, MLA_Q_RANK)), _full((1, MLA_KV_RANK)), _full((MLA_Q_RANK, hp)), _full((MLA_Q_RANK, hp)),
                  _full((MLA_KV_RANK, hp)), _full((hp, MLA_KV_RANK)), pos, pos],
        out_specs=[row(D_MODEL), row(C_NA), row(hp), row(hp), pl.BlockSpec((hp, tm), lambda i: (0, i))],
        out_shape=[jax.ShapeDtypeStruct((t, D_MODEL), F32), jax.ShapeDtypeStruct((t, C_NA), BF16),
                   jax.ShapeDtypeStruct((t, hp), BF16), jax.ShapeDtypeStruct((t, hp), BF16),
                   jax.ShapeDtypeStruct((hp, t), BF16)],
        compiler_params=_params("parallel"),
        name="proj",
    )(x2, g0, b0, w_in_ext, qg, kvg, wq_main, wq_sw, wk_pad, wv_pad_t, cos128, sin128)


def _na_bias_body(rpb_ref, o_ref):
    hh = pl.program_id(0)
    lane = lax.broadcasted_iota(jnp.int32, (GRID_W, LANES), 1)
    qc = lax.broadcasted_iota(jnp.int32, (GRID_W, LANES), 0)
    kc = lane % GRID_W
    first = lane < GRID_W
    dj = kc - qc + (NA_WIN_W - 1)
    cs = jnp.clip(qc - NA_WIN_W // 2, 0, GRID_W - NA_WIN_W)
    valid = (kc >= cs) & (kc < cs + NA_WIN_W)
    pair = []
    for a in range(NA_BIAS_ROWS - 1):
        acc = jnp.full((GRID_W, LANES), NEG, F32)
        for d in range(NA_BIAS_COLS):
            lo = rpb_ref[(hh * NA_BIAS_ROWS + a) * NA_BIAS_COLS + d]
            hi = rpb_ref[(hh * NA_BIAS_ROWS + a + 1) * NA_BIAS_COLS + d]
            acc = jnp.where(valid & (dj == d), jnp.where(first, lo, hi), acc)
        pair.append(acc)
    for d0 in range(NA_WIN_H):
        o_ref[0, d0] = jnp.concatenate([pair[d0 + 2 * i] for i in range(NA_WIN_H // 2)], axis=1)


def _na_bias(rpb_flat):
    band = NA_WIN_H * GRID_W
    return pl.pallas_call(
        _na_bias_body,
        grid=(NA_HEADS,),
        in_specs=[pl.BlockSpec(memory_space=pltpu.SMEM)],
        out_specs=pl.BlockSpec((1, NA_WIN_H, GRID_W, band), lambda h: (h, 0, 0, 0)),
        out_shape=jax.ShapeDtypeStruct((NA_HEADS, NA_WIN_H, GRID_W, band), F32),
        compiler_params=_params("arbitrary"),
        name="na_bias",
    )(rpb_flat)


NA_ROW_UNROLL = 2


def _na_body(q_ref, k_ref, v_ref, bias_ref, o_ref, *, rows):
    first = lax.broadcasted_iota(jnp.int32, (GRID_W, LANES), 1) < NA_HEAD_DIM
    band = NA_WIN_H * GRID_W

    def one_row(r, carry):
        rs = jnp.clip(r - NA_WIN_H // 2, 0, rows - NA_WIN_H)
        d0 = rs - r + (NA_WIN_H - 1)
        q0 = pl.multiple_of(r * GRID_W, GRID_W)
        k0 = pl.multiple_of(rs * GRID_W, GRID_W)
        pairs = range(NA_HEADS // 2)
        scores = []
        for j in pairs:
            cols = slice(j * LANES, (j + 1) * LANES)
            qp = q_ref[0, pl.ds(q0, GRID_W), cols]
            zero = jnp.zeros_like(qp)
            qs = jnp.concatenate([jnp.where(first, qp, zero), jnp.where(first, zero, qp)], axis=0)
            bias = jnp.concatenate([bias_ref[2 * j, d0], bias_ref[2 * j + 1, d0]], axis=0)
            scores.append(_mm_nt(qs, k_ref[0, pl.ds(k0, band), cols]) + bias)
        probs, sums = [], []
        for s in scores:
            p = jnp.exp(s - jnp.max(s, axis=-1, keepdims=True))
            sums.append(jnp.sum(p, axis=-1, keepdims=True))
            probs.append(p.astype(BF16))
        for j in pairs:
            cols = slice(j * LANES, (j + 1) * LANES)
            o = _mm(probs[j], v_ref[0, pl.ds(k0, band), cols]) / sums[j]
            o_ref[0, pl.ds(q0, GRID_W), cols] = jnp.where(first, o[:GRID_W], o[GRID_W:]).astype(BF16)
        return carry

    lax.fori_loop(0, rows, one_row, 0, unroll=NA_ROW_UNROLL)


def _na_attn(qkv, bias, seq):
    b = qkv.shape[0]
    rows = seq // GRID_W
    part = lambda c: pl.BlockSpec((1, seq, NA_WIDTH), lambda i: (i, 0, c))
    return pl.pallas_call(
        functools.partial(_na_body, rows=rows),
        grid=(b,),
        in_specs=[part(0), part(1), part(2), _full(bias.shape)],
        out_specs=pl.BlockSpec((1, seq, NA_WIDTH), lambda i: (i, 0, 0)),
        out_shape=jax.ShapeDtypeStruct((b, seq, NA_WIDTH), BF16),
        compiler_params=_params("parallel"),
        name="na_attn",
    )(qkv, qkv, qkv, bias)


def _mla_body(q_ref, k_ref, vt_ref, o_ref):
    def scores(j):
        out = []
        for e in range(2):
            cols = slice((2 * j + e) * LANES, (2 * j + e + 1) * LANES)
            out.append(_mm_nt(k_ref[0, :, cols], q_ref[0, :, cols]))
        return out

    pairs = MLA_HEADS // 2
    nxt = scores(0)
    for j in range(pairs):
        cur = nxt
        if j + 1 < pairs:
            nxt = scores(j + 1)
        acc = None
        for e in range(2):
            s = cur[e]
            p = jnp.exp(s - jnp.max(s, axis=0, keepdims=True))
            l = jnp.sum(p, axis=0, keepdims=True)
            rows = slice((2 * j + e) * LANES, (2 * j + e + 1) * LANES)
            o = _mm(vt_ref[rows, :], p.astype(BF16)) / l
            acc = o if acc is None else acc + o
        o_ref[0, :, j * LANES:(j + 1) * LANES] = acc.T.astype(BF16)


def _mla_attn(q, k, v_t, tq):
    b, seq, hp = q.shape
    width = MLA_HEADS * MLA_V
    return pl.pallas_call(
        _mla_body,
        grid=(b, seq // tq),
        in_specs=[pl.BlockSpec((1, tq, hp), lambda i, j: (i, j, 0)), pl.BlockSpec((1, seq, hp), lambda i, j: (i, 0, 0)),
                  pl.BlockSpec((hp, seq), lambda i, j: (0, i))],
        out_specs=pl.BlockSpec((1, tq, width), lambda i, j: (i, j, 0)),
        out_shape=jax.ShapeDtypeStruct((b, seq, width), BF16),
        compiler_params=_params("parallel", "arbitrary"),
        name="mla_attn",
    )(q, k, v_t)


def _top16(s, tag):
    big = jnp.int32(1 << 20)
    vals, tags = [], []
    for _ in range(PEER_TOPK):
        m = jnp.max(s, axis=0, keepdims=True)
        sel = jnp.min(jnp.where(s == m, tag, big), axis=0, keepdims=True)
        vals.append(m)
        tags.append(sel)
        s = jnp.where(tag == sel, -jnp.inf, s)
    return jnp.concatenate(vals, axis=0), jnp.concatenate(tags, axis=0)


_HALF_RANKS = PEER_TOPK // 2


def _head_topk(s1, s2):
    tm = s1.shape[1]
    key_id = lax.broadcasted_iota(jnp.int32, (PEER_KEYS, tm), 0)
    r16 = lax.broadcasted_iota(jnp.int32, (PEER_TOPK, tm), 0)
    r8 = lax.broadcasted_iota(jnp.int32, (_HALF_RANKS, tm), 0)
    flat = jnp.concatenate([r16] + [i * PEER_TOPK + r8 for i in range(1, _HALF_RANKS)]
                           + [(r8 + _HALF_RANKS) * PEER_TOPK], axis=0)
    v1, i1 = _top16(s1, key_id)
    v2, i2 = _top16(s2, key_id)
    cand = jnp.concatenate(
        [v1[0:1, :] + v2] + [v1[i:i + 1, :] + v2[:_HALF_RANKS, :] for i in range(1, _HALF_RANKS)]
        + [v1[_HALF_RANKS:, :] + v2[0:1, :]], axis=0)
    cv, ci = _top16(cand, flat)
    ci1 = lax.shift_right_logical(ci, 4)
    ci2 = lax.bitwise_and(ci, PEER_TOPK - 1)
    e1 = jnp.zeros((PEER_TOPK, tm), jnp.int32)
    e2 = jnp.zeros((PEER_TOPK, tm), jnp.int32)
    for r in range(PEER_TOPK):
        e1 = jnp.where(ci1 == r, i1[r:r + 1, :], e1)
        e2 = jnp.where(ci2 == r, i2[r:r + 1, :], e2)
    p = jnp.exp(cv - jnp.max(cv, axis=0, keepdims=True))
    return e1, e2, p / jnp.sum(p, axis=0, keepdims=True)


ROUTE_LANES = 256


def _mix_body(h_ref, na_ref, mla_ref, p_ref, wo_ref, g1_ref, b1_ref, wg_ref, bg_ref, wple_ref, wqt_ref, keys_ref,
              h1b_ref, r2_ref, e1_ref, e2_ref, g_ref):
    tm = h_ref.shape[0]
    mix = _mm(na_ref[...], wo_ref[:NA_WIDTH, :]) + _mm(mla_ref[...], wo_ref[NA_WIDTH:, :])
    h1 = _layer_norm(DN_ALPHA * h_ref[...] + mix, g1_ref[...], b1_ref[...])
    h1b = h1.astype(BF16)
    h1b_ref[...] = h1b
    gate = jax.nn.sigmoid(_mm(h1b, wg_ref[...]) + bg_ref[...])
    ple = gate * _mm(p_ref[...].astype(BF16), wple_ref[...])
    r2_ref[...] = DN_ALPHA * h1 + ple
    q_t = _mm_nt(wqt_ref[...], h1b).astype(BF16)
    e1s, e2s, gs = [], [], []
    for hh in range(PEER_HEADS):
        base = hh * 2 * PEER_HALF
        s1 = _mm(keys_ref[0], q_t[base:base + PEER_HALF, :])
        s2 = _mm(keys_ref[1], q_t[base + PEER_HALF:base + 2 * PEER_HALF, :])
        parts = [_head_topk(s1[:, c:c + ROUTE_LANES], s2[:, c:c + ROUTE_LANES]) for c in range(0, tm, ROUTE_LANES)]
        e1s.append(jnp.concatenate([p[0] for p in parts], axis=1))
        e2s.append(jnp.concatenate([p[1] for p in parts], axis=1))
        gs.append(jnp.concatenate([p[2] for p in parts], axis=1))
    e1_ref[...] = jnp.concatenate(e1s, axis=0).T
    e2_ref[...] = jnp.concatenate(e2s, axis=0).T
    g_ref[...] = jnp.concatenate(gs, axis=0).T


def _mix(h, a_na, a_mla, p2, w_o, g1, b1, w_g, b_g, w_ple, w_qt, keys, tm):
    t = h.shape[0]
    row = lambda w: pl.BlockSpec((tm, w), lambda i: (i, 0))
    nq = 2 * PEER_HEADS * PEER_HALF
    npair = PEER_HEADS * PEER_TOPK
    return pl.pallas_call(
        _mix_body,
        grid=(t // tm,),
        in_specs=[row(D_MODEL), row(NA_WIDTH), row(MLA_HEADS * MLA_V), row(PLE_DIM), _full((D_MODEL, D_MODEL)),
                  _full((1, D_MODEL)), _full((1, D_MODEL)), _full((D_MODEL, D_MODEL)), _full((1, D_MODEL)),
                  _full((PLE_DIM, D_MODEL)), _full((nq, D_MODEL)), _full((2, PEER_KEYS, PEER_HALF))],
        out_specs=[row(D_MODEL), row(D_MODEL), row(npair), row(npair), row(npair)],
        out_shape=[jax.ShapeDtypeStruct((t, D_MODEL), BF16), jax.ShapeDtypeStruct((t, D_MODEL), F32),
                   jax.ShapeDtypeStruct((t, npair), jnp.int32), jax.ShapeDtypeStruct((t, npair), jnp.int32),
                   jax.ShapeDtypeStruct((t, npair), F32)],
        compiler_params=_params("parallel"),
        name="mix",
    )(h, a_na, a_mla, p2, w_o, g1, b1, w_g, b_g, w_ple, w_qt, keys)


PEER_CHUNK = 32
WALL_PITCH = PEER_KEYS + 8


def _peer_body(x_ref, e1_ref, e2_ref, g_ref, u_ref, v_ref, r2_ref, g2_ref, b2_ref, o_ref, wall_ref, acc_ref):
    j = pl.program_id(1)
    tm = x_ref.shape[0]
    eb = u_ref.shape[0]
    npair = e1_ref.shape[1]

    @pl.when(j == 0)
    def _():
        acc_ref[...] = jnp.zeros_like(acc_ref)
        key_id = lax.broadcasted_iota(jnp.int32, (PEER_CHUNK, PEER_KEYS, 2 * npair), 1)

        def chunk(c, carry):
            rows = pl.ds(pl.multiple_of(c * PEER_CHUNK, PEER_CHUNK), PEER_CHUNK)
            e1 = e1_ref[rows, :]
            e2 = e2_ref[rows, :]
            g = g_ref[rows, :]
            g_hi = g.astype(BF16).astype(F32)
            gx = jnp.concatenate([g_hi, g - g_hi], axis=1)[:, None, :]
            e1x = jnp.concatenate([e1, e1], axis=1)[:, None, :]
            e2x = jnp.concatenate([e2, e2], axis=1)[:, None, :]
            a = jnp.where(e1x == key_id, 1.0, 0.0).astype(BF16)
            bm = jnp.where(e2x == key_id, gx, 0.0).astype(BF16)
            w = jnp.einsum("tep,tfp->tef", a, bm, preferred_element_type=F32)
            for i in range(PEER_CHUNK):
                base = pl.multiple_of((c * PEER_CHUNK + i) * WALL_PITCH, 8)
                wall_ref[pl.ds(base, PEER_KEYS), :] = w[i]
            return carry

        lax.fori_loop(0, tm // PEER_CHUNK, chunk, 0)

    nsub = eb // PEER_KEYS
    hpre = _mm_nt(x_ref[...], u_ref[...])
    w = jnp.concatenate([wall_ref[pl.ds(j * nsub + c, tm, stride=WALL_PITCH), :] for c in range(nsub)], axis=1)
    act = w * (0.5 * hpre * (1.0 + lax.erf(hpre * (2.0 ** -0.5))))
    acc_ref[...] += _mm(act.astype(BF16), v_ref[...])

    @pl.when(j == pl.num_programs(1) - 1)
    def _():
        o_ref[...] = _layer_norm(r2_ref[...] + acc_ref[...], g2_ref[...], b2_ref[...])


def _peer(h1b, e1, e2, g, u, v, r2, g2, b2, tm, eb):
    t = h1b.shape[0]
    n_exp = u.shape[0]
    npair = e1.shape[1]
    row = lambda w: pl.BlockSpec((tm, w), lambda i, j: (i, 0))
    tab = pl.BlockSpec((eb, D_MODEL), lambda i, j: (j, 0))
    vec = pl.BlockSpec((1, D_MODEL), lambda i, j: (0, 0))
    return pl.pallas_call(
        _peer_body,
        grid=(t // tm, n_exp // eb),
        in_specs=[row(D_MODEL), row(npair), row(npair), row(npair), tab, tab, row(D_MODEL), vec, vec],
        out_specs=row(D_MODEL),
        out_shape=jax.ShapeDtypeStruct((t, D_MODEL), F32),
        scratch_shapes=[pltpu.VMEM((tm * WALL_PITCH, PEER_KEYS), F32), pltpu.VMEM((tm, D_MODEL), F32)],
        compiler_params=_params("parallel", "arbitrary"),
        name="peer",
    )(h1b, e1, e2, g, u, v, r2, g2, b2)


def _rope_tables(seq):
    t = jnp.arange(seq)
    row = (t // GRID_W).astype(F32)
    col = (t % GRID_W).astype(F32)
    axis_dim = MLA_ROPE // 2
    inv = ROPE_BASE ** (-jnp.arange(0, axis_dim, 2, dtype=F32) / axis_dim)
    ang = jnp.concatenate([row[:, None] * inv[None, :], col[:, None] * inv[None, :]], axis=-1)
    cos, sin = jnp.cos(ang), jnp.sin(ang)
    pad = LANES - MLA_NOPE - MLA_ROPE
    cos128 = jnp.concatenate([jnp.ones((seq, MLA_NOPE), F32), cos, cos, jnp.zeros((seq, pad), F32)], axis=1)
    sin128 = jnp.concatenate([jnp.zeros((seq, MLA_NOPE), F32), -sin, sin, jnp.zeros((seq, pad), F32)], axis=1)
    return cos128, sin128


_PERM = np.concatenate([np.arange(0, MLA_ROPE, 2), np.arange(1, MLA_ROPE, 2)])
_PERM_SW = np.concatenate([np.arange(1, MLA_ROPE, 2), np.arange(0, MLA_ROPE, 2)])


def _prep_weights(w_in, w_uq, w_ukv):
    pad = LANES - MLA_NOPE - MLA_ROPE
    zin = lambda n: jnp.zeros((D_MODEL, n), F32)
    kr = w_in[:, C_CKV:]
    w_in_ext = jnp.concatenate(
        [w_in[:, :C_CKV], zin(MLA_NOPE), kr[:, _PERM], zin(pad), zin(MLA_NOPE), kr[:, _PERM_SW], zin(pad)], axis=1)
    wq = w_uq.reshape(MLA_Q_RANK, MLA_HEADS, MLA_QD)
    zq = lambda n: jnp.zeros((MLA_Q_RANK, MLA_HEADS, n), F32)
    rope = wq[:, :, MLA_NOPE:]
    wq_main = jnp.concatenate([wq[:, :, :MLA_NOPE], rope[:, :, _PERM], zq(pad)], axis=2)
    wq_sw = jnp.concatenate([zq(MLA_NOPE), rope[:, :, _PERM_SW], zq(pad)], axis=2)
    wkv = w_ukv.reshape(MLA_KV_RANK, MLA_HEADS, MLA_NOPE + MLA_V)
    zk = jnp.zeros((MLA_KV_RANK, MLA_HEADS, LANES - MLA_NOPE), F32)
    wk_pad = jnp.concatenate([wkv[:, :, :MLA_NOPE], zk], axis=2)
    vv = wkv[:, :, MLA_NOPE:]
    zv = jnp.zeros_like(vv)
    odd = (jnp.arange(MLA_HEADS) % 2 == 1)[None, :, None]
    wv_pad = jnp.where(odd, jnp.concatenate([zv, vv], axis=2), jnp.concatenate([vv, zv], axis=2))
    hp = MLA_HEADS * LANES
    flat = lambda w: w.reshape(w.shape[0], hp).astype(BF16)
    return w_in_ext.astype(BF16), flat(wq_main), flat(wq_sw), flat(wk_pad), flat(wv_pad).T


def kernel(x, p, emb_ln_g, emb_ln_b, w_in, mla_q_norm_g, mla_kv_norm_g, w_uq, w_ukv, na_rpb, w_o, ln1_g, ln1_b,
           peer_w_q, peer_sub_keys, peer_u, peer_v, ple_w, ple_gate_w, ple_gate_b, ln2_g, ln2_b):
    b, seq, d = x.shape
    assert d == D_MODEL and seq % GRID_W == 0 and w_in.shape[0] == DEPTH
    t = b * seq
    vec = lambda a: a.reshape(1, -1).astype(F32)
    w_in_ext, wq_main, wq_sw, wk_pad, wv_pad_t = _prep_weights(w_in[0], w_uq[0], w_ukv[0])
    cos128, sin128 = _rope_tables(seq)

    h, qkv_na, q_mla, k_mla, v_mla_t = _proj(
        x.reshape(t, d), vec(emb_ln_g), vec(emb_ln_b), w_in_ext, vec(mla_q_norm_g[0]), vec(mla_kv_norm_g[0]),
        wq_main, wq_sw, wk_pad, wv_pad_t, cos128, sin128, seq, tm=PROJ_TM)

    bias = _na_bias(na_rpb[0].reshape(-1).astype(F32))
    a_na = _na_attn(qkv_na.reshape(b, seq, C_NA), bias, seq)
    hp = MLA_HEADS * LANES
    a_mla = _mla_attn(q_mla.reshape(b, seq, hp), k_mla.reshape(b, seq, hp), v_mla_t, tq=MLA_TQ)

    h1b, r2, e1, e2, gates = _mix(
        h, a_na.reshape(t, NA_WIDTH), a_mla.reshape(t, MLA_HEADS * MLA_V), p[0].reshape(t, PLE_DIM),
        w_o[0].astype(BF16), vec(ln1_g[0]), vec(ln1_b[0]), ple_gate_w[0].astype(BF16), vec(ple_gate_b[0]),
        ple_w[0].astype(BF16), peer_w_q[0].T.astype(BF16), peer_sub_keys[0].astype(BF16), tm=MIX_TM)

    out = _peer(h1b, e1, e2, gates, peer_u[0].astype(BF16), peer_v[0].astype(BF16), r2,
                vec(ln2_g[0]), vec(ln2_b[0]), tm=PEER_TM, eb=PEER_EB)
    return out.reshape(b, seq, d)
```

```python
import functools

import numpy as np
import jax
import jax.numpy as jnp
from jax import lax
from jax.experimental import pallas as pl
from jax.experimental.pallas import tpu as pltpu

F32 = jnp.float32
BF16 = jnp.bfloat16

D_MODEL = 1024
GRID_W = 64
NA_HEADS = 8
NA_HEAD_DIM = 64
NA_WIN_H = 8
NA_WIN_W = 16
NA_WIDTH = NA_HEADS * NA_HEAD_DIM
NA_BIAS_ROWS = 2 * NA_WIN_H - 1
NA_BIAS_COLS = 2 * NA_WIN_W - 1
MLA_HEADS = 8
MLA_Q_RANK = 384
MLA_KV_RANK = 256
MLA_NOPE = 64
MLA_ROPE = 32
MLA_V = 64
MLA_QD = MLA_NOPE + MLA_ROPE
ROPE_BASE = 10000.0
PEER_HEADS = 8
PEER_KEYS = 128
PEER_HALF = 128
PEER_TOPK = 16
PLE_DIM = 256
DEPTH = 1
DN_ALPHA = float((2 * DEPTH) ** 0.25)
LN_EPS = 1e-5
LANES = 128
NEG = -1e30
VMEM_LIMIT = 56 * 1024 * 1024

PROJ_TM = 512
MLA_TQ = 512
MIX_TM = 512
PEER_TM = 256
PEER_EB = 2048

C_NA = 3 * NA_WIDTH
C_CQ = C_NA + MLA_Q_RANK
C_CKV = C_CQ + MLA_KV_RANK
C_KRA = C_CKV + LANES
C_IN = C_KRA + LANES

NT = (((1,), (1,)), ((), ()))


def _layer_norm(x, g, b):
    mu = jnp.mean(x, axis=-1, keepdims=True)
    xc = x - mu
    var = jnp.mean(xc * xc, axis=-1, keepdims=True)
    return xc * lax.rsqrt(var + LN_EPS) * g + b


def _rms_norm(x, g):
    return x * lax.rsqrt(jnp.mean(x * x, axis=-1, keepdims=True) + LN_EPS) * g


def _mm(a, b):
    return jnp.dot(a, b, preferred_element_type=F32)


def _mm_nt(a, b):
    return lax.dot_general(a, b, NT, preferred_element_type=F32)


def _params(*sem):
    return pltpu.CompilerParams(dimension_semantics=sem, vmem_limit_bytes=VMEM_LIMIT)


def _full(shape):
    return pl.BlockSpec(shape, lambda *_: (0,) * len(shape))


def _proj_body(x_ref, g0_ref, b0_ref, win_ref, qg_ref, kvg_ref, wqm_ref, wqs_ref, wk_ref, wv_ref,
               cos_ref, sin_ref, h_ref, na_ref, q_ref, k_ref, vt_ref):
    h = _layer_norm(x_ref[...], g0_ref[...], b0_ref[...])
    h_ref[...] = h
    z = _mm(h.astype(BF16), win_ref[...])
    na_ref[:, :NA_WIDTH] = (z[:, :NA_WIDTH] * (NA_HEAD_DIM ** -0.5)).astype(BF16)
    na_ref[:, NA_WIDTH:] = z[:, NA_WIDTH:C_NA].astype(BF16)
    cqn = _rms_norm(z[:, C_NA:C_CQ], qg_ref[...]).astype(BF16)
    ckvn = _rms_norm(z[:, C_CQ:C_CKV], kvg_ref[...]).astype(BF16)
    cos = cos_ref[...]
    sin = sin_ref[...]
    q = _mm(cqn, wqm_ref[...]) * jnp.tile(cos, (1, MLA_HEADS)) + _mm(cqn, wqs_ref[...]) * jnp.tile(sin, (1, MLA_HEADS))
    q_ref[...] = (q * (MLA_QD ** -0.5)).astype(BF16)
    k_rot = z[:, C_CKV:C_KRA] * cos + z[:, C_KRA:C_IN] * sin
    k_ref[...] = (_mm(ckvn, wk_ref[...]) + jnp.tile(k_rot, (1, MLA_HEADS))).astype(BF16)
    vt_ref[...] = _mm_nt(wv_ref[...], ckvn).astype(BF16)


def _proj(x2, g0, b0, w_in_ext, qg, kvg, wq_main, wq_sw, wk_pad, wv_pad_t, cos128, sin128, seq, tm):
    t = x2.shape[0]
    n_pos = seq // tm
    row = lambda w: pl.BlockSpec((tm, w), lambda i: (i, 0))
    pos = pl.BlockSpec((tm, LANES), lambda i: (i % n_pos, 0))
    hp = MLA_HEADS * LANES
    return pl.pallas_call(
        _proj_body,
        grid=(t // tm,),
        in_specs=[row(D_MODEL), _full((1, D_MODEL)), _full((1, D_MODEL)), _full((D_MODEL, C_IN)),
                  _full((1, MLA_Q_RANK)), _full((1, MLA_KV_RANK)), _full((MLA_Q_RANK, hp)), _full((MLA_Q_RANK, hp)),
                  _full((MLA_KV_RANK, hp)), _full((hp, MLA_KV_RANK)), pos, pos],
        out_specs=[row(D_MODEL), row(C_NA), row(hp), row(hp), pl.BlockSpec((hp, tm), lambda i: (0, i))],
        out_shape=[jax.ShapeDtypeStruct((t, D_MODEL), F32), jax.ShapeDtypeStruct((t, C_NA), BF16),
                   jax.ShapeDtypeStruct((t, hp), BF16), jax.ShapeDtypeStruct((t, hp), BF16),
                   jax.ShapeDtypeStruct((hp, t), BF16)],
        compiler_params=_params("parallel"),
        name="proj",
    )(x2, g0, b0, w_in_ext, qg, kvg, wq_main, wq_sw, wk_pad, wv_pad_t, cos128, sin128)


def _na_bias_body(rpb_ref, o_ref):
    hh = pl.program_id(0)
    lane = lax.broadcasted_iota(jnp.int32, (GRID_W, LANES), 1)
    qc = lax.broadcasted_iota(jnp.int32, (GRID_W, LANES), 0)
    kc = lane % GRID_W
    first = lane < GRID_W
    dj = kc - qc + (NA_WIN_W - 1)
    cs = jnp.clip(qc - NA_WIN_W // 2, 0, GRID_W - NA_WIN_W)
    valid = (kc >= cs) & (kc < cs + NA_WIN_W)
    pair = []
    for a in range(NA_BIAS_ROWS - 1):
        acc = jnp.full((GRID_W, LANES), NEG, F32)
        for d in range(NA_BIAS_COLS):
            lo = rpb_ref[(hh * NA_BIAS_ROWS + a) * NA_BIAS_COLS + d]
            hi = rpb_ref[(hh * NA_BIAS_ROWS + a + 1) * NA_BIAS_COLS + d]
            acc = jnp.where(valid & (dj == d), jnp.where(first, lo, hi), acc)
        pair.append(acc)
    for d0 in range(NA_WIN_H):
        o_ref[0, d0] = jnp.concatenate([pair[d0 + 2 * i] for i in range(NA_WIN_H // 2)], axis=1)


def _na_bias(rpb_flat):
    band = NA_WIN_H * GRID_W
    return pl.pallas_call(
        _na_bias_body,
        grid=(NA_HEADS,),
        in_specs=[pl.BlockSpec(memory_space=pltpu.SMEM)],
        out_specs=pl.BlockSpec((1, NA_WIN_H, GRID_W, band), lambda h: (h, 0, 0, 0)),
        out_shape=jax.ShapeDtypeStruct((NA_HEADS, NA_WIN_H, GRID_W, band), F32),
        compiler_params=_params("arbitrary"),
        name="na_bias",
    )(rpb_flat)


NA_ROW_UNROLL = 2


def _na_body(q_ref, k_ref, v_ref, bias_ref, o_ref, *, rows):
    first = lax.broadcasted_iota(jnp.int32, (GRID_W, LANES), 1) < NA_HEAD_DIM
    band = NA_WIN_H * GRID_W

    def one_row(r, carry):
        rs = jnp.clip(r - NA_WIN_H // 2, 0, rows - NA_WIN_H)
        d0 = rs - r + (NA_WIN_H - 1)
        q0 = pl.multiple_of(r * GRID_W, GRID_W)
        k0 = pl.multiple_of(rs * GRID_W, GRID_W)
        pairs = range(NA_HEADS // 2)
        scores = []
        for j in pairs:
            cols = slice(j * LANES, (j + 1) * LANES)
            qp = q_ref[0, pl.ds(q0, GRID_W), cols]
            zero = jnp.zeros_like(qp)
            qs = jnp.concatenate([jnp.where(first, qp, zero), jnp.where(first, zero, qp)], axis=0)
            bias = jnp.concatenate([bias_ref[2 * j, d0], bias_ref[2 * j + 1, d0]], axis=0)
            scores.append(_mm_nt(qs, k_ref[0, pl.ds(k0, band), cols]) + bias)
        probs, sums = [], []
        for s in scores:
            p = jnp.exp(s - jnp.max(s, axis=-1, keepdims=True))
            sums.append(jnp.sum(p, axis=-1, keepdims=True))
            probs.append(p.astype(BF16))
        for j in pairs:
            cols = slice(j * LANES, (j + 1) * LANES)
            o = _mm(probs[j], v_ref[0, pl.ds(k0, band), cols]) / sums[j]
            o_ref[0, pl.ds(q0, GRID_W), cols] = jnp.where(first, o[:GRID_W], o[GRID_W:]).astype(BF16)
        return carry

    lax.fori_loop(0, rows, one_row, 0, unroll=NA_ROW_UNROLL)


def _na_attn(qkv, bias, seq):
    b = qkv.shape[0]
    rows = seq // GRID_W
    part = lambda c: pl.BlockSpec((1, seq, NA_WIDTH), lambda i: (i, 0, c))
    return pl.pallas_call(
        functools.partial(_na_body, rows=rows),
        grid=(b,),
        in_specs=[part(0), part(1), part(2), _full(bias.shape)],
        out_specs=pl.BlockSpec((1, seq, NA_WIDTH), lambda i: (i, 0, 0)),
        out_shape=jax.ShapeDtypeStruct((b, seq, NA_WIDTH), BF16),
        compiler_params=_params("parallel"),
        name="na_attn",
    )(qkv, qkv, qkv, bias)


def _mla_body(q_ref, k_ref, vt_ref, o_ref):
    def scores(j):
        out = []
        for e in range(2):
            cols = slice((2 * j + e) * LANES, (2 * j + e + 1) * LANES)
            out.append(_mm_nt(k_ref[0, :, cols], q_ref[0, :, cols]))
        return out

    pairs = MLA_HEADS // 2
    nxt = scores(0)
    for j in range(pairs):
        cur = nxt
        if j + 1 < pairs:
            nxt = scores(j + 1)
        acc = None
        for e in range(2):
            s = cur[e]
            p = jnp.exp(s - jnp.max(s, axis=0, keepdims=True))
            l = jnp.sum(p, axis=0, keepdims=True)
            rows = slice((2 * j + e) * LANES, (2 * j + e + 1) * LANES)
            o = _mm(vt_ref[rows, :], p.astype(BF16)) / l
            acc = o if acc is None else acc + o
        o_ref[0, :, j * LANES:(j + 1) * LANES] = acc.T.astype(BF16)


def _mla_attn(q, k, v_t, tq):
    b, seq, hp = q.shape
    width = MLA_HEADS * MLA_V
    return pl.pallas_call(
        _mla_body,
        grid=(b, seq // tq),
        in_specs=[pl.BlockSpec((1, tq, hp), lambda i, j: (i, j, 0)), pl.BlockSpec((1, seq, hp), lambda i, j: (i, 0, 0)),
                  pl.BlockSpec((hp, seq), lambda i, j: (0, i))],
        out_specs=pl.BlockSpec((1, tq, width), lambda i, j: (i, j, 0)),
        out_shape=jax.ShapeDtypeStruct((b, seq, width), BF16),
        compiler_params=_params("parallel", "arbitrary"),
        name="mla_attn",
    )(q, k, v_t)


SUBLANES = 8


def _top16(rows, tags):
    big = jnp.int32(1 << 20)
    vals, sels = [], []
    for _ in range(PEER_TOPK):
        v, t = list(rows), list(tags)
        while len(v) > 1:
            nv, nt = [], []
            for a in range(0, len(v) - 1, 2):
                keep = v[a] >= v[a + 1]
                nv.append(jnp.where(keep, v[a], v[a + 1]))
                nt.append(jnp.where(keep, t[a], t[a + 1]))
            if len(v) % 2:
                nv.append(v[-1])
                nt.append(t[-1])
            v, t = nv, nt
        m = jnp.max(v[0], axis=0, keepdims=True)
        sel = jnp.min(jnp.where(v[0] == m, t[0], big), axis=0, keepdims=True)
        vals.append(m)
        sels.append(sel)
        rows = [jnp.where(tg == sel, -jnp.inf, r) for r, tg in zip(rows, tags)]
    return jnp.concatenate(vals, axis=0), jnp.concatenate(sels, axis=0)


def _tiles(x):
    return [x[r:r + SUBLANES, :] for r in range(0, x.shape[0], SUBLANES)]


_HALF_RANKS = PEER_TOPK // 2


def _head_topk(s1, s2):
    tm = s1.shape[1]
    r8 = lax.broadcasted_iota(jnp.int32, (_HALF_RANKS, tm), 0)
    key_tags = [r8 + SUBLANES * v for v in range(PEER_KEYS // SUBLANES)]
    v1, i1 = _top16(_tiles(s1), key_tags)
    v2, i2 = _top16(_tiles(s2), key_tags)
    cand = ([v1[0:1, :] + v2[:_HALF_RANKS, :], v1[0:1, :] + v2[_HALF_RANKS:, :]]
            + [v1[i:i + 1, :] + v2[:_HALF_RANKS, :] for i in range(1, _HALF_RANKS)]
            + [v1[_HALF_RANKS:, :] + v2[0:1, :]])
    flat = ([r8, r8 + _HALF_RANKS] + [i * PEER_TOPK + r8 for i in range(1, _HALF_RANKS)]
            + [(r8 + _HALF_RANKS) * PEER_TOPK])
    cv, ci = _top16(cand, flat)
    ci1 = lax.shift_right_logical(ci, 4)
    ci2 = lax.bitwise_and(ci, PEER_TOPK - 1)
    e1 = jnp.zeros((PEER_TOPK, tm), jnp.int32)
    e2 = jnp.zeros((PEER_TOPK, tm), jnp.int32)
    for r in range(PEER_TOPK):
        e1 = jnp.where(ci1 == r, i1[r:r + 1, :], e1)
        e2 = jnp.where(ci2 == r, i2[r:r + 1, :], e2)
    p = jnp.exp(cv - jnp.max(cv, axis=0, keepdims=True))
    return e1, e2, p / jnp.sum(p, axis=0, keepdims=True)


ROUTE_LANES = 256


def _mix_body(h_ref, na_ref, mla_ref, p_ref, wo_ref, g1_ref, b1_ref, wg_ref, bg_ref, wple_ref, wqt_ref, keys_ref,
              h1b_ref, r2_ref, e1_ref, e2_ref, g_ref):
    tm = h_ref.shape[0]
    mix = _mm(na_ref[...], wo_ref[:NA_WIDTH, :]) + _mm(mla_ref[...], wo_ref[NA_WIDTH:, :])
    h1 = _layer_norm(DN_ALPHA * h_ref[...] + mix, g1_ref[...], b1_ref[...])
    h1b = h1.astype(BF16)
    h1b_ref[...] = h1b
    gate = jax.nn.sigmoid(_mm(h1b, wg_ref[...]) + bg_ref[...])
    ple = gate * _mm(p_ref[...].astype(BF16), wple_ref[...])
    r2_ref[...] = DN_ALPHA * h1 + ple
    q_t = _mm_nt(wqt_ref[...], h1b).astype(BF16)
    e1s, e2s, gs = [], [], []
    for hh in range(PEER_HEADS):
        base = hh * 2 * PEER_HALF
        s1 = _mm(keys_ref[0], q_t[base:base + PEER_HALF, :])
        s2 = _mm(keys_ref[1], q_t[base + PEER_HALF:base + 2 * PEER_HALF, :])
        parts = [_head_topk(s1[:, c:c + ROUTE_LANES], s2[:, c:c + ROUTE_LANES]) for c in range(0, tm, ROUTE_LANES)]
        e1s.append(jnp.concatenate([p[0] for p in parts], axis=1))
        e2s.append(jnp.concatenate([p[1] for p in parts], axis=1))
        gs.append(jnp.concatenate([p[2] for p in parts], axis=1))
    e1_ref[...] = jnp.concatenate(e1s, axis=0).T
    e2_ref[...] = jnp.concatenate(e2s, axis=0).T
    g_ref[...] = jnp.concatenate(gs, axis=0).T


def _mix(h, a_na, a_mla, p2, w_o, g1, b1, w_g, b_g, w_ple, w_qt, keys, tm):
    t = h.shape[0]
    row = lambda w: pl.BlockSpec((tm, w), lambda i: (i, 0))
    nq = 2 * PEER_HEADS * PEER_HALF
    npair = PEER_HEADS * PEER_TOPK
    return pl.pallas_call(
        _mix_body,
        grid=(t // tm,),
        in_specs=[row(D_MODEL), row(NA_WIDTH), row(MLA_HEADS * MLA_V), row(PLE_DIM), _full((D_MODEL, D_MODEL)),
                  _full((1, D_MODEL)), _full((1, D_MODEL)), _full((D_MODEL, D_MODEL)), _full((1, D_MODEL)),
                  _full((PLE_DIM, D_MODEL)), _full((nq, D_MODEL)), _full((2, PEER_KEYS, PEER_HALF))],
        out_specs=[row(D_MODEL), row(D_MODEL), row(npair), row(npair), row(npair)],
        out_shape=[jax.ShapeDtypeStruct((t, D_MODEL), BF16), jax.ShapeDtypeStruct((t, D_MODEL), F32),
                   jax.ShapeDtypeStruct((t, npair), jnp.int32), jax.ShapeDtypeStruct((t, npair), jnp.int32),
                   jax.ShapeDtypeStruct((t, npair), F32)],
        compiler_params=_params("parallel"),
        name="mix",
    )(h, a_na, a_mla, p2, w_o, g1, b1, w_g, b_g, w_ple, w_qt, keys)


PEER_CHUNK = 32
WALL_PITCH = PEER_KEYS + 8


def _peer_body(x_ref, e1_ref, e2_ref, g_ref, u_ref, v_ref, r2_ref, g2_ref, b2_ref, o_ref, wall_ref, acc_ref):
    j = pl.program_id(1)
    tm = x_ref.shape[0]
    eb = u_ref.shape[0]
    npair = e1_ref.shape[1]

    @pl.when(j == 0)
    def _():
        acc_ref[...] = jnp.zeros_like(acc_ref)
        key_id = lax.broadcasted_iota(jnp.int32, (PEER_CHUNK, PEER_KEYS, 2 * npair), 1)

        def chunk(c, carry):
            rows = pl.ds(pl.multiple_of(c * PEER_CHUNK, PEER_CHUNK), PEER_CHUNK)
            e1 = e1_ref[rows, :]
            e2 = e2_ref[rows, :]
            g = g_ref[rows, :]
            g_hi = g.astype(BF16).astype(F32)
            gx = jnp.concatenate([g_hi, g - g_hi], axis=1)[:, None, :]
            e1x = jnp.concatenate([e1, e1], axis=1)[:, None, :]
            e2x = jnp.concatenate([e2, e2], axis=1)[:, None, :]
            a = jnp.where(e1x == key_id, 1.0, 0.0).astype(BF16)
            bm = jnp.where(e2x == key_id, gx, 0.0).astype(BF16)
            w = jnp.einsum("tep,tfp->tef", a, bm, preferred_element_type=F32)
            for i in range(PEER_CHUNK):
                base = pl.multiple_of((c * PEER_CHUNK + i) * WALL_PITCH, 8)
                wall_ref[pl.ds(base, PEER_KEYS), :] = w[i]
            return carry

        lax.fori_loop(0, tm // PEER_CHUNK, chunk, 0)

    nsub = eb // PEER_KEYS
    hpre = _mm_nt(x_ref[...], u_ref[...])
    w = jnp.concatenate([wall_ref[pl.ds(j * nsub + c, tm, stride=WALL_PITCH), :] for c in range(nsub)], axis=1)
    act = w * (0.5 * hpre * (1.0 + lax.erf(hpre * (2.0 ** -0.5))))
    acc_ref[...] += _mm(act.astype(BF16), v_ref[...])

    @pl.when(j == pl.num_programs(1) - 1)
    def _():
        o_ref[...] = _layer_norm(r2_ref[...] + acc_ref[...], g2_ref[...], b2_ref[...])


def _peer(h1b, e1, e2, g, u, v, r2, g2, b2, tm, eb):
    t = h1b.shape[0]
    n_exp = u.shape[0]
    npair = e1.shape[1]
    row = lambda w: pl.BlockSpec((tm, w), lambda i, j: (i, 0))
    tab = pl.BlockSpec((eb, D_MODEL), lambda i, j: (j, 0))
    vec = pl.BlockSpec((1, D_MODEL), lambda i, j: (0, 0))
    return pl.pallas_call(
        _peer_body,
        grid=(t // tm, n_exp // eb),
        in_specs=[row(D_MODEL), row(npair), row(npair), row(npair), tab, tab, row(D_MODEL), vec, vec],
        out_specs=row(D_MODEL),
        out_shape=jax.ShapeDtypeStruct((t, D_MODEL), F32),
        scratch_shapes=[pltpu.VMEM((tm * WALL_PITCH, PEER_KEYS), F32), pltpu.VMEM((tm, D_MODEL), F32)],
        compiler_params=_params("parallel", "arbitrary"),
        name="peer",
    )(h1b, e1, e2, g, u, v, r2, g2, b2)


def _rope_tables(seq):
    t = jnp.arange(seq)
    row = (t // GRID_W).astype(F32)
    col = (t % GRID_W).astype(F32)
    axis_dim = MLA_ROPE // 2
    inv = ROPE_BASE ** (-jnp.arange(0, axis_dim, 2, dtype=F32) / axis_dim)
    ang = jnp.concatenate([row[:, None] * inv[None, :], col[:, None] * inv[None, :]], axis=-1)
    cos, sin = jnp.cos(ang), jnp.sin(ang)
    pad = LANES - MLA_NOPE - MLA_ROPE
    cos128 = jnp.concatenate([jnp.ones((seq, MLA_NOPE), F32), cos, cos, jnp.zeros((seq, pad), F32)], axis=1)
    sin128 = jnp.concatenate([jnp.zeros((seq, MLA_NOPE), F32), -sin, sin, jnp.zeros((seq, pad), F32)], axis=1)
    return cos128, sin128


_PERM = np.concatenate([np.arange(0, MLA_ROPE, 2), np.arange(1, MLA_ROPE, 2)])
_PERM_SW = np.concatenate([np.arange(1, MLA_ROPE, 2), np.arange(0, MLA_ROPE, 2)])


def _prep_weights(w_in, w_uq, w_ukv):
    pad = LANES - MLA_NOPE - MLA_ROPE
    zin = lambda n: jnp.zeros((D_MODEL, n), F32)
    kr = w_in[:, C_CKV:]
    w_in_ext = jnp.concatenate(
        [w_in[:, :C_CKV], zin(MLA_NOPE), kr[:, _PERM], zin(pad), zin(MLA_NOPE), kr[:, _PERM_SW], zin(pad)], axis=1)
    wq = w_uq.reshape(MLA_Q_RANK, MLA_HEADS, MLA_QD)
    zq = lambda n: jnp.zeros((MLA_Q_RANK, MLA_HEADS, n), F32)
    rope = wq[:, :, MLA_NOPE:]
    wq_main = jnp.concatenate([wq[:, :, :MLA_NOPE], rope[:, :, _PERM], zq(pad)], axis=2)
    wq_sw = jnp.concatenate([zq(MLA_NOPE), rope[:, :, _PERM_SW], zq(pad)], axis=2)
    wkv = w_ukv.reshape(MLA_KV_RANK, MLA_HEADS, MLA_NOPE + MLA_V)
    zk = jnp.zeros((MLA_KV_RANK, MLA_HEADS, LANES - MLA_NOPE), F32)
    wk_pad = jnp.concatenate([wkv[:, :, :MLA_NOPE], zk], axis=2)
    vv = wkv[:, :, MLA_NOPE:]
    zv = jnp.zeros_like(vv)
    odd = (jnp.arange(MLA_HEADS) % 2 == 1)[None, :, None]
    wv_pad = jnp.where(odd, jnp.concatenate([zv, vv], axis=2), jnp.concatenate([vv, zv], axis=2))
    hp = MLA_HEADS * LANES
    flat = lambda w: w.reshape(w.shape[0], hp).astype(BF16)
    return w_in_ext.astype(BF16), flat(wq_main), flat(wq_sw), flat(wk_pad), flat(wv_pad).T


def kernel(x, p, emb_ln_g, emb_ln_b, w_in, mla_q_norm_g, mla_kv_norm_g, w_uq, w_ukv, na_rpb, w_o, ln1_g, ln1_b,
           peer_w_q, peer_sub_keys, peer_u, peer_v, ple_w, ple_gate_w, ple_gate_b, ln2_g, ln2_b):
    b, seq, d = x.shape
    assert d == D_MODEL and seq % GRID_W == 0 and w_in.shape[0] == DEPTH
    t = b * seq
    vec = lambda a: a.reshape(1, -1).astype(F32)
    w_in_ext, wq_main, wq_sw, wk_pad, wv_pad_t = _prep_weights(w_in[0], w_uq[0], w_ukv[0])
    cos128, sin128 = _rope_tables(seq)

    h, qkv_na, q_mla, k_mla, v_mla_t = _proj(
        x.reshape(t, d), vec(emb_ln_g), vec(emb_ln_b), w_in_ext, vec(mla_q_norm_g[0]), vec(mla_kv_norm_g[0]),
        wq_main, wq_sw, wk_pad, wv_pad_t, cos128, sin128, seq, tm=PROJ_TM)

    bias = _na_bias(na_rpb[0].reshape(-1).astype(F32))
    a_na = _na_attn(qkv_na.reshape(b, seq, C_NA), bias, seq)
    hp = MLA_HEADS * LANES
    a_mla = _mla_attn(q_mla.reshape(b, seq, hp), k_mla.reshape(b, seq, hp), v_mla_t, tq=MLA_TQ)

    h1b, r2, e1, e2, gates = _mix(
        h, a_na.reshape(t, NA_WIDTH), a_mla.reshape(t, MLA_HEADS * MLA_V), p[0].reshape(t, PLE_DIM),
        w_o[0].astype(BF16), vec(ln1_g[0]), vec(ln1_b[0]), ple_gate_w[0].astype(BF16), vec(ple_gate_b[0]),
        ple_w[0].astype(BF16), peer_w_q[0].T.astype(BF16), peer_sub_keys[0].astype(BF16), tm=MIX_TM)

    out = _peer(h1b, e1, e2, gates, peer_u[0].astype(BF16), peer_v[0].astype(BF16), r2,
                vec(ln2_g[0]), vec(ln2_b[0]), tm=PEER_TM, eb=PEER_EB)
    return out.reshape(b, seq, d)
```

```python
import functools

import numpy as np
import jax
import jax.numpy as jnp
from jax import lax
from jax.experimental import pallas as pl
from jax.experimental.pallas import tpu as pltpu

F32 = jnp.float32
BF16 = jnp.bfloat16

D_MODEL = 1024
GRID_W = 64
NA_HEADS = 8
NA_HEAD_DIM = 64
NA_WIN_H = 8
NA_WIN_W = 16
NA_WIDTH = NA_HEADS * NA_HEAD_DIM
NA_BIAS_ROWS = 2 * NA_WIN_H - 1
NA_BIAS_COLS = 2 * NA_WIN_W - 1
MLA_HEADS = 8
MLA_Q_RANK = 384
MLA_KV_RANK = 256
MLA_NOPE = 64
MLA_ROPE = 32
MLA_V = 64
MLA_QD = MLA_NOPE + MLA_ROPE
ROPE_BASE = 10000.0
PEER_HEADS = 8
PEER_KEYS = 128
PEER_HALF = 128
PEER_TOPK = 16
PLE_DIM = 256
DEPTH = 1
DN_ALPHA = float((2 * DEPTH) ** 0.25)
LN_EPS = 1e-5
LANES = 128
NEG = -1e30
VMEM_LIMIT = 58 * 1024 * 1024

PROJ_TM = 512
MLA_TQ = 512
MIX_TM = 512
PEER_TM = 512
PEER_EB = 1024

C_NA = 3 * NA_WIDTH
C_CQ = C_NA + MLA_Q_RANK
C_CKV = C_CQ + MLA_KV_RANK
C_KRA = C_CKV + LANES
C_IN = C_KRA + LANES

NT = (((1,), (1,)), ((), ()))


def _layer_norm(x, g, b):
    mu = jnp.mean(x, axis=-1, keepdims=True)
    xc = x - mu
    var = jnp.mean(xc * xc, axis=-1, keepdims=True)
    return xc * lax.rsqrt(var + LN_EPS) * g + b


def _rms_norm(x, g):
    return x * lax.rsqrt(jnp.mean(x * x, axis=-1, keepdims=True) + LN_EPS) * g


def _mm(a, b):
    return jnp.dot(a, b, preferred_element_type=F32)


def _mm_nt(a, b):
    return lax.dot_general(a, b, NT, preferred_element_type=F32)


def _params(*sem):
    return pltpu.CompilerParams(dimension_semantics=sem, vmem_limit_bytes=VMEM_LIMIT)


def _full(shape):
    return pl.BlockSpec(shape, lambda *_: (0,) * len(shape))


def _proj_body(x_ref, g0_ref, b0_ref, win_ref, qg_ref, kvg_ref, wqm_ref, wqs_ref, wk_ref, wv_ref,
               cos_ref, sin_ref, h_ref, na_ref, q_ref, k_ref, vt_ref):
    h = _layer_norm(x_ref[...], g0_ref[...], b0_ref[...])
    h_ref[...] = h
    z = _mm(h.astype(BF16), win_ref[...])
    na_ref[:, :NA_WIDTH] = (z[:, :NA_WIDTH] * (NA_HEAD_DIM ** -0.5)).astype(BF16)
    na_ref[:, NA_WIDTH:] = z[:, NA_WIDTH:C_NA].astype(BF16)
    cqn = _rms_norm(z[:, C_NA:C_CQ], qg_ref[...]).astype(BF16)
    ckvn = _rms_norm(z[:, C_CQ:C_CKV], kvg_ref[...]).astype(BF16)
    cos = cos_ref[...]
    sin = sin_ref[...]
    q = _mm(cqn, wqm_ref[...]) * jnp.tile(cos, (1, MLA_HEADS)) + _mm(cqn, wqs_ref[...]) * jnp.tile(sin, (1, MLA_HEADS))
    q_ref[...] = (q * (MLA_QD ** -0.5)).astype(BF16)
    k_rot = z[:, C_CKV:C_KRA] * cos + z[:, C_KRA:C_IN] * sin
    k_ref[...] = (_mm(ckvn, wk_ref[...]) + jnp.tile(k_rot, (1, MLA_HEADS))).astype(BF16)
    vt_ref[...] = _mm_nt(wv_ref[...], ckvn).astype(BF16)


def _proj(x2, g0, b0, w_in_ext, qg, kvg, wq_main, wq_sw, wk_pad, wv_pad_t, cos128, sin128, seq, tm):
    t = x2.shape[0]
    n_pos = seq // tm
    row = lambda w: pl.BlockSpec((tm, w), lambda i: (i, 0))
    pos = pl.BlockSpec((tm, LANES), lambda i: (i % n_pos, 0))
    hp = MLA_HEADS * LANES
    return pl.pallas_call(
        _proj_body,
        grid=(t // tm,),
        in_specs=[row(D_MODEL), _full((1, D_MODEL)), _full((1, D_MODEL)), _full((D_MODEL, C_IN)),
                  _full((1, MLA_Q_RANK)), _full((1, MLA_KV_RANK)), _full((MLA_Q_RANK, hp)), _full((MLA_Q_RANK, hp)),
                  _full((MLA_KV_RANK, hp)), _full((hp, MLA_KV_RANK)), pos, pos],
        out_specs=[row(D_MODEL), row(C_NA), row(hp), row(hp), pl.BlockSpec((hp, tm), lambda i: (0, i))],
        out_shape=[jax.ShapeDtypeStruct((t, D_MODEL), F32), jax.ShapeDtypeStruct((t, C_NA), BF16),
                   jax.ShapeDtypeStruct((t, hp), BF16), jax.ShapeDtypeStruct((t, hp), BF16),
                   jax.ShapeDtypeStruct((hp, t), BF16)],
        compiler_params=_params("parallel"),
        name="proj",
    )(x2, g0, b0, w_in_ext, qg, kvg, wq_main, wq_sw, wk_pad, wv_pad_t, cos128, sin128)


def _na_bias_body(rpb_ref, o_ref):
    hh = pl.program_id(0)
    lane = lax.broadcasted_iota(jnp.int32, (GRID_W, LANES), 1)
    qc = lax.broadcasted_iota(jnp.int32, (GRID_W, LANES), 0)
    kc = lane % GRID_W
    first = lane < GRID_W
    dj = kc - qc + (NA_WIN_W - 1)
    cs = jnp.clip(qc - NA_WIN_W // 2, 0, GRID_W - NA_WIN_W)
    valid = (kc >= cs) & (kc < cs + NA_WIN_W)
    pair = []
    for a in range(NA_BIAS_ROWS - 1):
        acc = jnp.full((GRID_W, LANES), NEG, F32)
        for d in range(NA_BIAS_COLS):
            lo = rpb_ref[(hh * NA_BIAS_ROWS + a) * NA_BIAS_COLS + d]
            hi = rpb_ref[(hh * NA_BIAS_ROWS + a + 1) * NA_BIAS_COLS + d]
            acc = jnp.where(valid & (dj == d), jnp.where(first, lo, hi), acc)
        pair.append(acc)
    for d0 in range(NA_WIN_H):
        o_ref[0, d0] = jnp.concatenate([pair[d0 + 2 * i] for i in range(NA_WIN_H // 2)], axis=1)


def _na_bias(rpb_flat):
    band = NA_WIN_H * GRID_W
    return pl.pallas_call(
        _na_bias_body,
        grid=(NA_HEADS,),
        in_specs=[pl.BlockSpec(memory_space=pltpu.SMEM)],
        out_specs=pl.BlockSpec((1, NA_WIN_H, GRID_W, band), lambda h: (h, 0, 0, 0)),
        out_shape=jax.ShapeDtypeStruct((NA_HEADS, NA_WIN_H, GRID_W, band), F32),
        compiler_params=_params("arbitrary"),
        name="na_bias",
    )(rpb_flat)


NA_ROW_UNROLL = 2


def _na_body(q_ref, k_ref, v_ref, bias_ref, o_ref, *, rows):
    first = lax.broadcasted_iota(jnp.int32, (GRID_W, LANES), 1) < NA_HEAD_DIM
    band = NA_WIN_H * GRID_W

    def one_row(r, carry):
        rs = jnp.clip(r - NA_WIN_H // 2, 0, rows - NA_WIN_H)
        d0 = rs - r + (NA_WIN_H - 1)
        q0 = pl.multiple_of(r * GRID_W, GRID_W)
        k0 = pl.multiple_of(rs * GRID_W, GRID_W)
        pairs = range(NA_HEADS // 2)
        scores = []
        for j in pairs:
            cols = slice(j * LANES, (j + 1) * LANES)
            qp = q_ref[0, pl.ds(q0, GRID_W), cols]
            zero = jnp.zeros_like(qp)
            qs = jnp.concatenate([jnp.where(first, qp, zero), jnp.where(first, zero, qp)], axis=0)
            bias = jnp.concatenate([bias_ref[2 * j, d0], bias_ref[2 * j + 1, d0]], axis=0)
            scores.append(_mm_nt(qs, k_ref[0, pl.ds(k0, band), cols]) + bias)
        probs, sums = [], []
        for s in scores:
            p = jnp.exp(s - jnp.max(s, axis=-1, keepdims=True))
            sums.append(jnp.sum(p, axis=-1, keepdims=True))
            probs.append(p.astype(BF16))
        for j in pairs:
            cols = slice(j * LANES, (j + 1) * LANES)
            o = _mm(probs[j], v_ref[0, pl.ds(k0, band), cols]) / sums[j]
            o_ref[0, pl.ds(q0, GRID_W), cols] = jnp.where(first, o[:GRID_W], o[GRID_W:]).astype(BF16)
        return carry

    lax.fori_loop(0, rows, one_row, 0, unroll=NA_ROW_UNROLL)


def _na_attn(qkv, bias, seq):
    b = qkv.shape[0]
    rows = seq // GRID_W
    part = lambda c: pl.BlockSpec((1, seq, NA_WIDTH), lambda i: (i, 0, c))
    return pl.pallas_call(
        functools.partial(_na_body, rows=rows),
        grid=(b,),
        in_specs=[part(0), part(1), part(2), _full(bias.shape)],
        out_specs=pl.BlockSpec((1, seq, NA_WIDTH), lambda i: (i, 0, 0)),
        out_shape=jax.ShapeDtypeStruct((b, seq, NA_WIDTH), BF16),
        compiler_params=_params("parallel"),
        name="na_attn",
    )(qkv, qkv, qkv, bias)


def _mla_body(q_ref, k_ref, vt_ref, o_ref):
    def scores(j):
        out = []
        for e in range(2):
            cols = slice((2 * j + e) * LANES, (2 * j + e + 1) * LANES)
            out.append(_mm_nt(k_ref[0, :, cols], q_ref[0, :, cols]))
        return out

    pairs = MLA_HEADS // 2
    nxt = scores(0)
    for j in range(pairs):
        cur = nxt
        if j + 1 < pairs:
            nxt = scores(j + 1)
        acc = None
        for e in range(2):
            s = cur[e]
            p = jnp.exp(s - jnp.max(s, axis=0, keepdims=True))
            l = jnp.sum(p, axis=0, keepdims=True)
            rows = slice((2 * j + e) * LANES, (2 * j + e + 1) * LANES)
            o = _mm(vt_ref[rows, :], p.astype(BF16)) / l
            acc = o if acc is None else acc + o
        o_ref[0, :, j * LANES:(j + 1) * LANES] = acc.T.astype(BF16)


def _mla_attn(q, k, v_t, tq):
    b, seq, hp = q.shape
    width = MLA_HEADS * MLA_V
    return pl.pallas_call(
        _mla_body,
        grid=(b, seq // tq),
        in_specs=[pl.BlockSpec((1, tq, hp), lambda i, j: (i, j, 0)), pl.BlockSpec((1, seq, hp), lambda i, j: (i, 0, 0)),
                  pl.BlockSpec((hp, seq), lambda i, j: (0, i))],
        out_specs=pl.BlockSpec((1, tq, width), lambda i, j: (i, j, 0)),
        out_shape=jax.ShapeDtypeStruct((b, seq, width), BF16),
        compiler_params=_params("parallel", "arbitrary"),
        name="mla_attn",
    )(q, k, v_t)


SUBLANES = 8


def _top16(rows, tags):
    big = jnp.int32(1 << 20)
    vals, sels = [], []
    for _ in range(PEER_TOPK):
        v, t = list(rows), list(tags)
        while len(v) > 1:
            nv, nt = [], []
            for a in range(0, len(v) - 1, 2):
                keep = v[a] >= v[a + 1]
                nv.append(jnp.where(keep, v[a], v[a + 1]))
                nt.append(jnp.where(keep, t[a], t[a + 1]))
            if len(v) % 2:
                nv.append(v[-1])
                nt.append(t[-1])
            v, t = nv, nt
        m = jnp.max(v[0], axis=0, keepdims=True)
        sel = jnp.min(jnp.where(v[0] == m, t[0], big), axis=0, keepdims=True)
        vals.append(m)
        sels.append(sel)
        rows = [jnp.where(tg == sel, -jnp.inf, r) for r, tg in zip(rows, tags)]
    return jnp.concatenate(vals, axis=0), jnp.concatenate(sels, axis=0)


def _tiles(x):
    return [x[r:r + SUBLANES, :] for r in range(0, x.shape[0], SUBLANES)]


_HALF_RANKS = PEER_TOPK // 2


def _head_topk(s1, s2):
    tm = s1.shape[1]
    r8 = lax.broadcasted_iota(jnp.int32, (_HALF_RANKS, tm), 0)
    key_tags = [r8 + SUBLANES * v for v in range(PEER_KEYS // SUBLANES)]
    v1, i1 = _top16(_tiles(s1), key_tags)
    v2, i2 = _top16(_tiles(s2), key_tags)
    cand = ([v1[0:1, :] + v2[:_HALF_RANKS, :], v1[0:1, :] + v2[_HALF_RANKS:, :]]
            + [v1[i:i + 1, :] + v2[:_HALF_RANKS, :] for i in range(1, _HALF_RANKS)]
            + [v1[_HALF_RANKS:, :] + v2[0:1, :]])
    flat = ([r8, r8 + _HALF_RANKS] + [i * PEER_TOPK + r8 for i in range(1, _HALF_RANKS)]
            + [(r8 + _HALF_RANKS) * PEER_TOPK])
    cv, ci = _top16(cand, flat)
    ci1 = lax.shift_right_logical(ci, 4)
    ci2 = lax.bitwise_and(ci, PEER_TOPK - 1)
    e1 = jnp.zeros((PEER_TOPK, tm), jnp.int32)
    e2 = jnp.zeros((PEER_TOPK, tm), jnp.int32)
    for r in range(PEER_TOPK):
        e1 = jnp.where(ci1 == r, i1[r:r + 1, :], e1)
        e2 = jnp.where(ci2 == r, i2[r:r + 1, :], e2)
    p = jnp.exp(cv - jnp.max(cv, axis=0, keepdims=True))
    return e1, e2, p / jnp.sum(p, axis=0, keepdims=True)


ROUTE_LANES = 256


def _mix_body(h_ref, na_ref, mla_ref, p_ref, wo_ref, g1_ref, b1_ref, wg_ref, bg_ref, wple_ref, wqt_ref, keys_ref,
              h1b_ref, r2_ref, e1_ref, e2_ref, g_ref):
    tm = h_ref.shape[0]
    mix = _mm(na_ref[...], wo_ref[:NA_WIDTH, :]) + _mm(mla_ref[...], wo_ref[NA_WIDTH:, :])
    h1 = _layer_norm(DN_ALPHA * h_ref[...] + mix, g1_ref[...], b1_ref[...])
    h1b = h1.astype(BF16)
    h1b_ref[...] = h1b
    gate = jax.nn.sigmoid(_mm(h1b, wg_ref[...]) + bg_ref[...])
    ple = gate * _mm(p_ref[...].astype(BF16), wple_ref[...])
    r2_ref[...] = DN_ALPHA * h1 + ple
    q_t = _mm_nt(wqt_ref[...], h1b).astype(BF16)
    e1s, e2s, gs = [], [], []
    for hh in range(PEER_HEADS):
        base = hh * 2 * PEER_HALF
        s1 = _mm(keys_ref[0], q_t[base:base + PEER_HALF, :])
        s2 = _mm(keys_ref[1], q_t[base + PEER_HALF:base + 2 * PEER_HALF, :])
        parts = [_head_topk(s1[:, c:c + ROUTE_LANES], s2[:, c:c + ROUTE_LANES]) for c in range(0, tm, ROUTE_LANES)]
        e1s.append(jnp.concatenate([p[0] for p in parts], axis=1))
        e2s.append(jnp.concatenate([p[1] for p in parts], axis=1))
        gs.append(jnp.concatenate([p[2] for p in parts], axis=1))
    e1_ref[...] = jnp.concatenate(e1s, axis=0).T
    e2_ref[...] = jnp.concatenate(e2s, axis=0).T
    g_ref[...] = jnp.concatenate(gs, axis=0).T


def _mix(h, a_na, a_mla, p2, w_o, g1, b1, w_g, b_g, w_ple, w_qt, keys, tm):
    t = h.shape[0]
    row = lambda w: pl.BlockSpec((tm, w), lambda i: (i, 0))
    nq = 2 * PEER_HEADS * PEER_HALF
    npair = PEER_HEADS * PEER_TOPK
    return pl.pallas_call(
        _mix_body,
        grid=(t // tm,),
        in_specs=[row(D_MODEL), row(NA_WIDTH), row(MLA_HEADS * MLA_V), row(PLE_DIM), _full((D_MODEL, D_MODEL)),
                  _full((1, D_MODEL)), _full((1, D_MODEL)), _full((D_MODEL, D_MODEL)), _full((1, D_MODEL)),
                  _full((PLE_DIM, D_MODEL)), _full((nq, D_MODEL)), _full((2, PEER_KEYS, PEER_HALF))],
        out_specs=[row(D_MODEL), row(D_MODEL), row(npair), row(npair), row(npair)],
        out_shape=[jax.ShapeDtypeStruct((t, D_MODEL), BF16), jax.ShapeDtypeStruct((t, D_MODEL), F32),
                   jax.ShapeDtypeStruct((t, npair), jnp.int32), jax.ShapeDtypeStruct((t, npair), jnp.int32),
                   jax.ShapeDtypeStruct((t, npair), F32)],
        compiler_params=_params("parallel"),
        name="mix",
    )(h, a_na, a_mla, p2, w_o, g1, b1, w_g, b_g, w_ple, w_qt, keys)


PEER_CHUNK = 64
WALL_PITCH = PEER_KEYS + 8


def _peer_body(x_ref, e1_ref, e2_ref, g_ref, u_ref, v_ref, r2_ref, g2_ref, b2_ref, o_ref, wall_ref):
    j = pl.program_id(1)
    tm = x_ref.shape[0]
    eb = u_ref.shape[0]
    npair = e1_ref.shape[1]

    @pl.when(j == 0)
    def _():
        o_ref[...] = jnp.zeros_like(o_ref)
        key_id = lax.broadcasted_iota(jnp.int32, (PEER_CHUNK, PEER_KEYS, 2 * npair), 1)

        def chunk(c, carry):
            rows = pl.ds(pl.multiple_of(c * PEER_CHUNK, PEER_CHUNK), PEER_CHUNK)
            e1 = e1_ref[rows, :]
            e2 = e2_ref[rows, :]
            g = g_ref[rows, :]
            g_hi = g.astype(BF16).astype(F32)
            gx = jnp.concatenate([g_hi, g - g_hi], axis=1)[:, None, :]
            e1x = jnp.concatenate([e1, e1], axis=1)[:, None, :]
            e2x = jnp.concatenate([e2, e2], axis=1)[:, None, :]
            a = jnp.where(e1x == key_id, 1.0, 0.0).astype(BF16)
            bm = jnp.where(e2x == key_id, gx, 0.0).astype(BF16)
            w = jnp.einsum("tep,tfp->tef", a, bm, preferred_element_type=F32)
            for i in range(PEER_CHUNK):
                base = pl.multiple_of((c * PEER_CHUNK + i) * WALL_PITCH, 8)
                wall_ref[pl.ds(base, PEER_KEYS), :] = w[i]
            return carry

        lax.fori_loop(0, tm // PEER_CHUNK, chunk, 0)

    nsub = eb // PEER_KEYS
    hpre = _mm_nt(x_ref[...], u_ref[...])
    w = jnp.concatenate([wall_ref[pl.ds(j * nsub + c, tm, stride=WALL_PITCH), :] for c in range(nsub)], axis=1)
    act = w * (0.5 * hpre * (1.0 + lax.erf(hpre * (2.0 ** -0.5))))
    o_ref[...] += _mm(act.astype(BF16), v_ref[...])

    @pl.when(j == pl.num_programs(1) - 1)
    def _():
        o_ref[...] = _layer_norm(r2_ref[...] + o_ref[...], g2_ref[...], b2_ref[...])


def _peer(h1b, e1, e2, g, u, v, r2, g2, b2, tm, eb):
    t = h1b.shape[0]
    n_exp = u.shape[0]
    npair = e1.shape[1]
    row = lambda w: pl.BlockSpec((tm, w), lambda i, j: (i, 0))
    tab = pl.BlockSpec((eb, D_MODEL), lambda i, j: (j, 0))
    vec = pl.BlockSpec((1, D_MODEL), lambda i, j: (0, 0))
    resid = pl.BlockSpec((tm, D_MODEL), lambda i, j: (i, 0), pipeline_mode=pl.Buffered(1))
    return pl.pallas_call(
        _peer_body,
        grid=(t // tm, n_exp // eb),
        in_specs=[row(D_MODEL), row(npair), row(npair), row(npair), tab, tab, resid, vec, vec],
        out_specs=row(D_MODEL),
        out_shape=jax.ShapeDtypeStruct((t, D_MODEL), F32),
        scratch_shapes=[pltpu.VMEM((tm * WALL_PITCH, PEER_KEYS), F32)],
        compiler_params=_params("parallel", "arbitrary"),
        name="peer",
    )(h1b, e1, e2, g, u, v, r2, g2, b2)


def _rope_tables(seq):
    t = jnp.arange(seq)
    row = (t // GRID_W).astype(F32)
    col = (t % GRID_W).astype(F32)
    axis_dim = MLA_ROPE // 2
    inv = ROPE_BASE ** (-jnp.arange(0, axis_dim, 2, dtype=F32) / axis_dim)
    ang = jnp.concatenate([row[:, None] * inv[None, :], col[:, None] * inv[None, :]], axis=-1)
    cos, sin = jnp.cos(ang), jnp.sin(ang)
    pad = LANES - MLA_NOPE - MLA_ROPE
    cos128 = jnp.concatenate([jnp.ones((seq, MLA_NOPE), F32), cos, cos, jnp.zeros((seq, pad), F32)], axis=1)
    sin128 = jnp.concatenate([jnp.zeros((seq, MLA_NOPE), F32), -sin, sin, jnp.zeros((seq, pad), F32)], axis=1)
    return cos128, sin128


_PERM = np.concatenate([np.arange(0, MLA_ROPE, 2), np.arange(1, MLA_ROPE, 2)])
_PERM_SW = np.concatenate([np.arange(1, MLA_ROPE, 2), np.arange(0, MLA_ROPE, 2)])


def _prep_weights(w_in, w_uq, w_ukv):
    pad = LANES - MLA_NOPE - MLA_ROPE
    zin = lambda n: jnp.zeros((D_MODEL, n), F32)
    kr = w_in[:, C_CKV:]
    w_in_ext = jnp.concatenate(
        [w_in[:, :C_CKV], zin(MLA_NOPE), kr[:, _PERM], zin(pad), zin(MLA_NOPE), kr[:, _PERM_SW], zin(pad)], axis=1)
    wq = w_uq.reshape(MLA_Q_RANK, MLA_HEADS, MLA_QD)
    zq = lambda n: jnp.zeros((MLA_Q_RANK, MLA_HEADS, n), F32)
    rope = wq[:, :, MLA_NOPE:]
    wq_main = jnp.concatenate([wq[:, :, :MLA_NOPE], rope[:, :, _PERM], zq(pad)], axis=2)
    wq_sw = jnp.concatenate([zq(MLA_NOPE), rope[:, :, _PERM_SW], zq(pad)], axis=2)
    wkv = w_ukv.reshape(MLA_KV_RANK, MLA_HEADS, MLA_NOPE + MLA_V)
    zk = jnp.zeros((MLA_KV_RANK, MLA_HEADS, LANES - MLA_NOPE), F32)
    wk_pad = jnp.concatenate([wkv[:, :, :MLA_NOPE], zk], axis=2)
    vv = wkv[:, :, MLA_NOPE:]
    zv = jnp.zeros_like(vv)
    odd = (jnp.arange(MLA_HEADS) % 2 == 1)[None, :, None]
    wv_pad = jnp.where(odd, jnp.concatenate([zv, vv], axis=2), jnp.concatenate([vv, zv], axis=2))
    hp = MLA_HEADS * LANES
    flat = lambda w: w.reshape(w.shape[0], hp).astype(BF16)
    return w_in_ext.astype(BF16), flat(wq_main), flat(wq_sw), flat(wk_pad), flat(wv_pad).T


def kernel(x, p, emb_ln_g, emb_ln_b, w_in, mla_q_norm_g, mla_kv_norm_g, w_uq, w_ukv, na_rpb, w_o, ln1_g, ln1_b,
           peer_w_q, peer_sub_keys, peer_u, peer_v, ple_w, ple_gate_w, ple_gate_b, ln2_g, ln2_b):
    b, seq, d = x.shape
    assert d == D_MODEL and seq % GRID_W == 0 and w_in.shape[0] == DEPTH
    t = b * seq
    vec = lambda a: a.reshape(1, -1).astype(F32)
    w_in_ext, wq_main, wq_sw, wk_pad, wv_pad_t = _prep_weights(w_in[0], w_uq[0], w_ukv[0])
    cos128, sin128 = _rope_tables(seq)

    h, qkv_na, q_mla, k_mla, v_mla_t = _proj(
        x.reshape(t, d), vec(emb_ln_g), vec(emb_ln_b), w_in_ext, vec(mla_q_norm_g[0]), vec(mla_kv_norm_g[0]),
        wq_main, wq_sw, wk_pad, wv_pad_t, cos128, sin128, seq, tm=PROJ_TM)

    bias = _na_bias(na_rpb[0].reshape(-1).astype(F32))
    a_na = _na_attn(qkv_na.reshape(b, seq, C_NA), bias, seq)
    hp = MLA_HEADS * LANES
    a_mla = _mla_attn(q_mla.reshape(b, seq, hp), k_mla.reshape(b, seq, hp), v_mla_t, tq=MLA_TQ)

    h1b, r2, e1, e2, gates = _mix(
        h, a_na.reshape(t, NA_WIDTH), a_mla.reshape(t, MLA_HEADS * MLA_V), p[0].reshape(t, PLE_DIM),
        w_o[0].astype(BF16), vec(ln1_g[0]), vec(ln1_b[0]), ple_gate_w[0].astype(BF16), vec(ple_gate_b[0]),
        ple_w[0].astype(BF16), peer_w_q[0].T.astype(BF16), peer_sub_keys[0].astype(BF16), tm=MIX_TM)

    out = _peer(h1b, e1, e2, gates, peer_u[0].astype(BF16), peer_v[0].astype(BF16), r2,
                vec(ln2_g[0]), vec(ln2_b[0]), tm=PEER_TM, eb=PEER_EB)
    return out.reshape(b, seq, d)
```

```python
import functools

import numpy as np
import jax
import jax.numpy as jnp
from jax import lax
from jax.experimental import pallas as pl
from jax.experimental.pallas import tpu as pltpu

F32 = jnp.float32
BF16 = jnp.bfloat16

D_MODEL = 1024
GRID_W = 64
NA_HEADS = 8
NA_HEAD_DIM = 64
NA_WIN_H = 8
NA_WIN_W = 16
NA_WIDTH = NA_HEADS * NA_HEAD_DIM
NA_BIAS_ROWS = 2 * NA_WIN_H - 1
NA_BIAS_COLS = 2 * NA_WIN_W - 1
MLA_HEADS = 8
MLA_Q_RANK = 384
MLA_KV_RANK = 256
MLA_NOPE = 64
MLA_ROPE = 32
MLA_V = 64
MLA_QD = MLA_NOPE + MLA_ROPE
ROPE_BASE = 10000.0
PEER_HEADS = 8
PEER_KEYS = 128
PEER_HALF = 128
PEER_TOPK = 16
PLE_DIM = 256
DEPTH = 1
DN_ALPHA = float((2 * DEPTH) ** 0.25)
LN_EPS = 1e-5
LANES = 128
NEG = -1e30
VMEM_LIMIT = 58 * 1024 * 1024

PROJ_TM = 512
MLA_TQ = 512
MIX_TM = 512
PEER_TM = 512
PEER_EB = 1024

C_NA = 3 * NA_WIDTH
C_CQ = C_NA + MLA_Q_RANK
C_CKV = C_CQ + MLA_KV_RANK
C_KRA = C_CKV + LANES
C_IN = C_KRA + LANES

NT = (((1,), (1,)), ((), ()))


def _layer_norm(x, g, b):
    mu = jnp.mean(x, axis=-1, keepdims=True)
    xc = x - mu
    var = jnp.mean(xc * xc, axis=-1, keepdims=True)
    return xc * lax.rsqrt(var + LN_EPS) * g + b


def _rms_norm(x, g):
    return x * lax.rsqrt(jnp.mean(x * x, axis=-1, keepdims=True) + LN_EPS) * g


def _mm(a, b):
    return jnp.dot(a, b, preferred_element_type=F32)


def _mm_nt(a, b):
    return lax.dot_general(a, b, NT, preferred_element_type=F32)


def _params(*sem):
    return pltpu.CompilerParams(dimension_semantics=sem, vmem_limit_bytes=VMEM_LIMIT)


def _full(shape):
    return pl.BlockSpec(shape, lambda *_: (0,) * len(shape))


def _proj_body(x_ref, g0_ref, b0_ref, win_ref, qg_ref, kvg_ref, wqm_ref, wqs_ref, wk_ref, wv_ref,
               cos_ref, sin_ref, h_ref, na_ref, q_ref, k_ref, vt_ref):
    h = _layer_norm(x_ref[...], g0_ref[...], b0_ref[...])
    h_ref[...] = h
    z = _mm(h.astype(BF16), win_ref[...])
    na_ref[:, :NA_WIDTH] = (z[:, :NA_WIDTH] * (NA_HEAD_DIM ** -0.5)).astype(BF16)
    na_ref[:, NA_WIDTH:] = z[:, NA_WIDTH:C_NA].astype(BF16)
    cqn = _rms_norm(z[:, C_NA:C_CQ], qg_ref[...]).astype(BF16)
    ckvn = _rms_norm(z[:, C_CQ:C_CKV], kvg_ref[...]).astype(BF16)
    cos = cos_ref[...]
    sin = sin_ref[...]
    q = _mm(cqn, wqm_ref[...]) * jnp.tile(cos, (1, MLA_HEADS)) + _mm(cqn, wqs_ref[...]) * jnp.tile(sin, (1, MLA_HEADS))
    q_ref[...] = (q * (MLA_QD ** -0.5)).astype(BF16)
    k_rot = z[:, C_CKV:C_KRA] * cos + z[:, C_KRA:C_IN] * sin
    k_ref[...] = (_mm(ckvn, wk_ref[...]) + jnp.tile(k_rot, (1, MLA_HEADS))).astype(BF16)
    vt_ref[...] = _mm_nt(wv_ref[...], ckvn).astype(BF16)


def _proj(x2, g0, b0, w_in_ext, qg, kvg, wq_main, wq_sw, wk_pad, wv_pad_t, cos128, sin128, seq, tm):
    t = x2.shape[0]
    n_pos = seq // tm
    row = lambda w: pl.BlockSpec((tm, w), lambda i: (i, 0))
    pos = pl.BlockSpec((tm, LANES), lambda i: (i % n_pos, 0))
    hp = MLA_HEADS * LANES
    return pl.pallas_call(
        _proj_body,
        grid=(t // tm,),
        in_specs=[row(D_MODEL), _full((1, D_MODEL)), _full((1, D_MODEL)), _full((D_MODEL, C_IN)),
                  _full((1, MLA_Q_RANK)), _full((1, MLA_KV_RANK)), _full((MLA_Q_RANK, hp)), _full((MLA_Q_RANK, hp)),
                  _full((MLA_KV_RANK, hp)), _full((hp, MLA_KV_RANK)), pos, pos],
        out_specs=[row(D_MODEL), row(C_NA), row(hp), row(hp), pl.BlockSpec((hp, tm), lambda i: (0, i))],
        out_shape=[jax.ShapeDtypeStruct((t, D_MODEL), F32), jax.ShapeDtypeStruct((t, C_NA), BF16),
                   jax.ShapeDtypeStruct((t, hp), BF16), jax.ShapeDtypeStruct((t, hp), BF16),
                   jax.ShapeDtypeStruct((hp, t), BF16)],
        compiler_params=_params("parallel"),
        name="proj",
    )(x2, g0, b0, w_in_ext, qg, kvg, wq_main, wq_sw, wk_pad, wv_pad_t, cos128, sin128)


def _na_bias_body(rpb_ref, o_ref):
    hh = pl.program_id(0)
    lane = lax.broadcasted_iota(jnp.int32, (GRID_W, LANES), 1)
    qc = lax.broadcasted_iota(jnp.int32, (GRID_W, LANES), 0)
    kc = lane % GRID_W
    first = lane < GRID_W
    dj = kc - qc + (NA_WIN_W - 1)
    cs = jnp.clip(qc - NA_WIN_W // 2, 0, GRID_W - NA_WIN_W)
    valid = (kc >= cs) & (kc < cs + NA_WIN_W)
    pair = []
    for a in range(NA_BIAS_ROWS - 1):
        acc = jnp.full((GRID_W, LANES), NEG, F32)
        for d in range(NA_BIAS_COLS):
            lo = rpb_ref[(hh * NA_BIAS_ROWS + a) * NA_BIAS_COLS + d]
            hi = rpb_ref[(hh * NA_BIAS_ROWS + a + 1) * NA_BIAS_COLS + d]
            acc = jnp.where(valid & (dj == d), jnp.where(first, lo, hi), acc)
        pair.append(acc)
    for d0 in range(NA_WIN_H):
        o_ref[0, d0] = jnp.concatenate([pair[d0 + 2 * i] for i in range(NA_WIN_H // 2)], axis=1)


def _na_bias(rpb_flat):
    band = NA_WIN_H * GRID_W
    return pl.pallas_call(
        _na_bias_body,
        grid=(NA_HEADS,),
        in_specs=[pl.BlockSpec(memory_space=pltpu.SMEM)],
        out_specs=pl.BlockSpec((1, NA_WIN_H, GRID_W, band), lambda h: (h, 0, 0, 0)),
        out_shape=jax.ShapeDtypeStruct((NA_HEADS, NA_WIN_H, GRID_W, band), F32),
        compiler_params=_params("arbitrary"),
        name="na_bias",
    )(rpb_flat)


NA_ROW_UNROLL = 2


def _na_body(q_ref, k_ref, v_ref, bias_ref, o_ref, *, rows):
    first = lax.broadcasted_iota(jnp.int32, (GRID_W, LANES), 1) < NA_HEAD_DIM
    band = NA_WIN_H * GRID_W

    def one_row(r, carry):
        rs = jnp.clip(r - NA_WIN_H // 2, 0, rows - NA_WIN_H)
        d0 = rs - r + (NA_WIN_H - 1)
        q0 = pl.multiple_of(r * GRID_W, GRID_W)
        k0 = pl.multiple_of(rs * GRID_W, GRID_W)
        pairs = range(NA_HEADS // 2)
        scores = []
        for j in pairs:
            cols = slice(j * LANES, (j + 1) * LANES)
            qp = q_ref[0, pl.ds(q0, GRID_W), cols]
            zero = jnp.zeros_like(qp)
            qs = jnp.concatenate([jnp.where(first, qp, zero), jnp.where(first, zero, qp)], axis=0)
            bias = jnp.concatenate([bias_ref[2 * j, d0], bias_ref[2 * j + 1, d0]], axis=0)
            scores.append(_mm_nt(qs, k_ref[0, pl.ds(k0, band), cols]) + bias)
        probs, sums = [], []
        for s in scores:
            p = jnp.exp(s - jnp.max(s, axis=-1, keepdims=True))
            sums.append(jnp.sum(p, axis=-1, keepdims=True))
            probs.append(p.astype(BF16))
        for j in pairs:
            cols = slice(j * LANES, (j + 1) * LANES)
            o = _mm(probs[j], v_ref[0, pl.ds(k0, band), cols]) / sums[j]
            o_ref[0, pl.ds(q0, GRID_W), cols] = jnp.where(first, o[:GRID_W], o[GRID_W:]).astype(BF16)
        return carry

    lax.fori_loop(0, rows, one_row, 0, unroll=NA_ROW_UNROLL)


def _na_attn(qkv, bias, seq):
    b = qkv.shape[0]
    rows = seq // GRID_W
    part = lambda c: pl.BlockSpec((1, seq, NA_WIDTH), lambda i: (i, 0, c))
    return pl.pallas_call(
        functools.partial(_na_body, rows=rows),
        grid=(b,),
        in_specs=[part(0), part(1), part(2), _full(bias.shape)],
        out_specs=pl.BlockSpec((1, seq, NA_WIDTH), lambda i: (i, 0, 0)),
        out_shape=jax.ShapeDtypeStruct((b, seq, NA_WIDTH), BF16),
        compiler_params=_params("parallel"),
        name="na_attn",
    )(qkv, qkv, qkv, bias)


def _mla_body(q_ref, k_ref, vt_ref, o_ref):
    def scores(j):
        out = []
        for e in range(2):
            cols = slice((2 * j + e) * LANES, (2 * j + e + 1) * LANES)
            out.append(_mm_nt(k_ref[0, :, cols], q_ref[0, :, cols]))
        return out

    pairs = MLA_HEADS // 2
    nxt = scores(0)
    for j in range(pairs):
        cur = nxt
        if j + 1 < pairs:
            nxt = scores(j + 1)
        acc = None
        for e in range(2):
            s = cur[e]
            p = jnp.exp(s - jnp.max(s, axis=0, keepdims=True))
            l = jnp.sum(p, axis=0, keepdims=True)
            rows = slice((2 * j + e) * LANES, (2 * j + e + 1) * LANES)
            o = _mm(vt_ref[rows, :], p.astype(BF16)) / l
            acc = o if acc is None else acc + o
        o_ref[0, :, j * LANES:(j + 1) * LANES] = acc.T.astype(BF16)


def _mla_attn(q, k, v_t, tq):
    b, seq, hp = q.shape
    width = MLA_HEADS * MLA_V
    return pl.pallas_call(
        _mla_body,
        grid=(b, seq // tq),
        in_specs=[pl.BlockSpec((1, tq, hp), lambda i, j: (i, j, 0)), pl.BlockSpec((1, seq, hp), lambda i, j: (i, 0, 0)),
                  pl.BlockSpec((hp, seq), lambda i, j: (0, i))],
        out_specs=pl.BlockSpec((1, tq, width), lambda i, j: (i, j, 0)),
        out_shape=jax.ShapeDtypeStruct((b, seq, width), BF16),
        compiler_params=_params("parallel", "arbitrary"),
        name="mla_attn",
    )(q, k, v_t)


SUBLANES = 8


def _top16(rows, tags):
    big = jnp.int32(1 << 20)
    vals, sels = [], []
    for _ in range(PEER_TOPK):
        v, t = list(rows), list(tags)
        while len(v) > 1:
            nv, nt = [], []
            for a in range(0, len(v) - 1, 2):
                keep = v[a] >= v[a + 1]
                nv.append(jnp.where(keep, v[a], v[a + 1]))
                nt.append(jnp.where(keep, t[a], t[a + 1]))
            if len(v) % 2:
                nv.append(v[-1])
                nt.append(t[-1])
            v, t = nv, nt
        m = jnp.max(v[0], axis=0, keepdims=True)
        sel = jnp.min(jnp.where(v[0] == m, t[0], big), axis=0, keepdims=True)
        vals.append(m)
        sels.append(sel)
        rows = [jnp.where(tg == sel, -jnp.inf, r) for r, tg in zip(rows, tags)]
    return jnp.concatenate(vals, axis=0), jnp.concatenate(sels, axis=0)


COLUMN_DEPTH = 8


def _top16_columns(rows, tags):
    groups = []
    for g in range(0, len(rows), COLUMN_DEPTH):
        v, t = list(rows[g:g + COLUMN_DEPTH]), list(tags[g:g + COLUMN_DEPTH])
        for rnd in range(COLUMN_DEPTH):
            for a in range(rnd % 2, COLUMN_DEPTH - 1, 2):
                swap = v[a + 1] > v[a]
                v[a], v[a + 1] = jnp.maximum(v[a], v[a + 1]), jnp.minimum(v[a], v[a + 1])
                t[a], t[a + 1] = jnp.where(swap, t[a + 1], t[a]), jnp.where(swap, t[a], t[a + 1])
        groups.append((v, t))
    big = jnp.int32(1 << 20)
    vals, sels = [], []
    for extraction in range(PEER_TOPK):
        hv, ht = [v[0] for v, _ in groups], [t[0] for _, t in groups]
        while len(hv) > 1:
            nv, nt = [], []
            for a in range(0, len(hv) - 1, 2):
                keep = hv[a] >= hv[a + 1]
                nv.append(jnp.where(keep, hv[a], hv[a + 1]))
                nt.append(jnp.where(keep, ht[a], ht[a + 1]))
            if len(hv) % 2:
                nv.append(hv[-1])
                nt.append(ht[-1])
            hv, ht = nv, nt
        m = jnp.max(hv[0], axis=0, keepdims=True)
        sel = jnp.min(jnp.where(hv[0] == m, ht[0], big), axis=0, keepdims=True)
        vals.append(m)
        sels.append(sel)
        live = min(COLUMN_DEPTH, PEER_TOPK - extraction)
        for v, t in groups:
            pop = t[0] == sel
            for k in range(live - 1):
                v[k] = jnp.where(pop, v[k + 1], v[k])
                t[k] = jnp.where(pop, t[k + 1], t[k])
            if live == COLUMN_DEPTH:
                v[live - 1] = jnp.where(pop, -jnp.inf, v[live - 1])
    return jnp.concatenate(vals, axis=0), jnp.concatenate(sels, axis=0)


def _tiles(x):
    return [x[r:r + SUBLANES, :] for r in range(0, x.shape[0], SUBLANES)]


_HALF_RANKS = PEER_TOPK // 2


def _head_topk(s1, s2):
    tm = s1.shape[1]
    r8 = lax.broadcasted_iota(jnp.int32, (_HALF_RANKS, tm), 0)
    key_tags = [r8 + SUBLANES * v for v in range(PEER_KEYS // SUBLANES)]
    v1, i1 = _top16_columns(_tiles(s1), key_tags)
    v2, i2 = _top16_columns(_tiles(s2), key_tags)
    cand = ([v1[0:1, :] + v2[:_HALF_RANKS, :], v1[0:1, :] + v2[_HALF_RANKS:, :]]
            + [v1[i:i + 1, :] + v2[:_HALF_RANKS, :] for i in range(1, _HALF_RANKS)]
            + [v1[_HALF_RANKS:, :] + v2[0:1, :]])
    flat = ([r8, r8 + _HALF_RANKS] + [i * PEER_TOPK + r8 for i in range(1, _HALF_RANKS)]
            + [(r8 + _HALF_RANKS) * PEER_TOPK])
    cv, ci = _top16(cand, flat)
    ci1 = lax.shift_right_logical(ci, 4)
    ci2 = lax.bitwise_and(ci, PEER_TOPK - 1)
    e1 = jnp.zeros((PEER_TOPK, tm), jnp.int32)
    e2 = jnp.zeros((PEER_TOPK, tm), jnp.int32)
    for r in range(PEER_TOPK):
        e1 = jnp.where(ci1 == r, i1[r:r + 1, :], e1)
        e2 = jnp.where(ci2 == r, i2[r:r + 1, :], e2)
    p = jnp.exp(cv - jnp.max(cv, axis=0, keepdims=True))
    return e1, e2, p / jnp.sum(p, axis=0, keepdims=True)


ROUTE_LANES = 256


def _mix_body(h_ref, na_ref, mla_ref, p_ref, wo_ref, g1_ref, b1_ref, wg_ref, bg_ref, wple_ref, wqt_ref, keys_ref,
              h1b_ref, r2_ref, e1_ref, e2_ref, g_ref):
    tm = h_ref.shape[0]
    mix = _mm(na_ref[...], wo_ref[:NA_WIDTH, :]) + _mm(mla_ref[...], wo_ref[NA_WIDTH:, :])
    h1 = _layer_norm(DN_ALPHA * h_ref[...] + mix, g1_ref[...], b1_ref[...])
    h1b = h1.astype(BF16)
    h1b_ref[...] = h1b
    gate = jax.nn.sigmoid(_mm(h1b, wg_ref[...]) + bg_ref[...])
    ple = gate * _mm(p_ref[...].astype(BF16), wple_ref[...])
    r2_ref[...] = DN_ALPHA * h1 + ple
    q_t = _mm_nt(wqt_ref[...], h1b).astype(BF16)
    e1s, e2s, gs = [], [], []
    for hh in range(PEER_HEADS):
        base = hh * 2 * PEER_HALF
        s1 = _mm(keys_ref[0], q_t[base:base + PEER_HALF, :])
        s2 = _mm(keys_ref[1], q_t[base + PEER_HALF:base + 2 * PEER_HALF, :])
        parts = [_head_topk(s1[:, c:c + ROUTE_LANES], s2[:, c:c + ROUTE_LANES]) for c in range(0, tm, ROUTE_LANES)]
        e1s.append(jnp.concatenate([p[0] for p in parts], axis=1))
        e2s.append(jnp.concatenate([p[1] for p in parts], axis=1))
        gs.append(jnp.concatenate([p[2] for p in parts], axis=1))
    e1_ref[...] = jnp.concatenate(e1s, axis=0).T
    e2_ref[...] = jnp.concatenate(e2s, axis=0).T
    g_ref[...] = jnp.concatenate(gs, axis=0).T


def _mix(h, a_na, a_mla, p2, w_o, g1, b1, w_g, b_g, w_ple, w_qt, keys, tm):
    t = h.shape[0]
    row = lambda w: pl.BlockSpec((tm, w), lambda i: (i, 0))
    nq = 2 * PEER_HEADS * PEER_HALF
    npair = PEER_HEADS * PEER_TOPK
    return pl.pallas_call(
        _mix_body,
        grid=(t // tm,),
        in_specs=[row(D_MODEL), row(NA_WIDTH), row(MLA_HEADS * MLA_V), row(PLE_DIM), _full((D_MODEL, D_MODEL)),
                  _full((1, D_MODEL)), _full((1, D_MODEL)), _full((D_MODEL, D_MODEL)), _full((1, D_MODEL)),
                  _full((PLE_DIM, D_MODEL)), _full((nq, D_MODEL)), _full((2, PEER_KEYS, PEER_HALF))],
        out_specs=[row(D_MODEL), row(D_MODEL), row(npair), row(npair), row(npair)],
        out_shape=[jax.ShapeDtypeStruct((t, D_MODEL), BF16), jax.ShapeDtypeStruct((t, D_MODEL), F32),
                   jax.ShapeDtypeStruct((t, npair), jnp.int32), jax.ShapeDtypeStruct((t, npair), jnp.int32),
                   jax.ShapeDtypeStruct((t, npair), F32)],
        compiler_params=_params("parallel"),
        name="mix",
    )(h, a_na, a_mla, p2, w_o, g1, b1, w_g, b_g, w_ple, w_qt, keys)


PEER_CHUNK = 64
WALL_PITCH = PEER_KEYS + 8


def _peer_body(x_ref, e1_ref, e2_ref, g_ref, u_ref, v_ref, r2_ref, g2_ref, b2_ref, o_ref, wall_ref):
    j = pl.program_id(1)
    tm = x_ref.shape[0]
    eb = u_ref.shape[0]
    npair = e1_ref.shape[1]

    @pl.when(j == 0)
    def _():
        o_ref[...] = jnp.zeros_like(o_ref)
        key_id = lax.broadcasted_iota(jnp.int32, (PEER_CHUNK, PEER_KEYS, 2 * npair), 1)

        def chunk(c, carry):
            rows = pl.ds(pl.multiple_of(c * PEER_CHUNK, PEER_CHUNK), PEER_CHUNK)
            e1 = e1_ref[rows, :]
            e2 = e2_ref[rows, :]
            g = g_ref[rows, :]
            g_hi = g.astype(BF16).astype(F32)
            gx = jnp.concatenate([g_hi, g - g_hi], axis=1)[:, None, :]
            e1x = jnp.concatenate([e1, e1], axis=1)[:, None, :]
            e2x = jnp.concatenate([e2, e2], axis=1)[:, None, :]
            a = jnp.where(e1x == key_id, 1.0, 0.0).astype(BF16)
            bm = jnp.where(e2x == key_id, gx, 0.0).astype(BF16)
            w = jnp.einsum("tep,tfp->tef", a, bm, preferred_element_type=F32)
            for i in range(PEER_CHUNK):
                base = pl.multiple_of((c * PEER_CHUNK + i) * WALL_PITCH, 8)
                wall_ref[pl.ds(base, PEER_KEYS), :] = w[i]
            return carry

        lax.fori_loop(0, tm // PEER_CHUNK, chunk, 0)

    nsub = eb // PEER_KEYS
    hpre = _mm_nt(x_ref[...], u_ref[...])
    w = jnp.concatenate([wall_ref[pl.ds(j * nsub + c, tm, stride=WALL_PITCH), :] for c in range(nsub)], axis=1)
    act = w * (0.5 * hpre * (1.0 + lax.erf(hpre * (2.0 ** -0.5))))
    o_ref[...] += _mm(act.astype(BF16), v_ref[...])

    @pl.when(j == pl.num_programs(1) - 1)
    def _():
        o_ref[...] = _layer_norm(r2_ref[...] + o_ref[...], g2_ref[...], b2_ref[...])


def _peer(h1b, e1, e2, g, u, v, r2, g2, b2, tm, eb):
    t = h1b.shape[0]
    n_exp = u.shape[0]
    npair = e1.shape[1]
    row = lambda w: pl.BlockSpec((tm, w), lambda i, j: (i, 0))
    tab = pl.BlockSpec((eb, D_MODEL), lambda i, j: (j, 0))
    vec = pl.BlockSpec((1, D_MODEL), lambda i, j: (0, 0))
    resid = pl.BlockSpec((tm, D_MODEL), lambda i, j: (i, 0), pipeline_mode=pl.Buffered(1))
    return pl.pallas_call(
        _peer_body,
        grid=(t // tm, n_exp // eb),
        in_specs=[row(D_MODEL), row(npair), row(npair), row(npair), tab, tab, resid, vec, vec],
        out_specs=row(D_MODEL),
        out_shape=jax.ShapeDtypeStruct((t, D_MODEL), F32),
        scratch_shapes=[pltpu.VMEM((tm * WALL_PITCH, PEER_KEYS), F32)],
        compiler_params=_params("parallel", "arbitrary"),
        name="peer",
    )(h1b, e1, e2, g, u, v, r2, g2, b2)


def _rope_tables(seq):
    t = jnp.arange(seq)
    row = (t // GRID_W).astype(F32)
    col = (t % GRID_W).astype(F32)
    axis_dim = MLA_ROPE // 2
    inv = ROPE_BASE ** (-jnp.arange(0, axis_dim, 2, dtype=F32) / axis_dim)
    ang = jnp.concatenate([row[:, None] * inv[None, :], col[:, None] * inv[None, :]], axis=-1)
    cos, sin = jnp.cos(ang), jnp.sin(ang)
    pad = LANES - MLA_NOPE - MLA_ROPE
    cos128 = jnp.concatenate([jnp.ones((seq, MLA_NOPE), F32), cos, cos, jnp.zeros((seq, pad), F32)], axis=1)
    sin128 = jnp.concatenate([jnp.zeros((seq, MLA_NOPE), F32), -sin, sin, jnp.zeros((seq, pad), F32)], axis=1)
    return cos128, sin128


_PERM = np.concatenate([np.arange(0, MLA_ROPE, 2), np.arange(1, MLA_ROPE, 2)])
_PERM_SW = np.concatenate([np.arange(1, MLA_ROPE, 2), np.arange(0, MLA_ROPE, 2)])


def _prep_weights(w_in, w_uq, w_ukv):
    pad = LANES - MLA_NOPE - MLA_ROPE
    zin = lambda n: jnp.zeros((D_MODEL, n), F32)
    kr = w_in[:, C_CKV:]
    w_in_ext = jnp.concatenate(
        [w_in[:, :C_CKV], zin(MLA_NOPE), kr[:, _PERM], zin(pad), zin(MLA_NOPE), kr[:, _PERM_SW], zin(pad)], axis=1)
    wq = w_uq.reshape(MLA_Q_RANK, MLA_HEADS, MLA_QD)
    zq = lambda n: jnp.zeros((MLA_Q_RANK, MLA_HEADS, n), F32)
    rope = wq[:, :, MLA_NOPE:]
    wq_main = jnp.concatenate([wq[:, :, :MLA_NOPE], rope[:, :, _PERM], zq(pad)], axis=2)
    wq_sw = jnp.concatenate([zq(MLA_NOPE), rope[:, :, _PERM_SW], zq(pad)], axis=2)
    wkv = w_ukv.reshape(MLA_KV_RANK, MLA_HEADS, MLA_NOPE + MLA_V)
    zk = jnp.zeros((MLA_KV_RANK, MLA_HEADS, LANES - MLA_NOPE), F32)
    wk_pad = jnp.concatenate([wkv[:, :, :MLA_NOPE], zk], axis=2)
    vv = wkv[:, :, MLA_NOPE:]
    zv = jnp.zeros_like(vv)
    odd = (jnp.arange(MLA_HEADS) % 2 == 1)[None, :, None]
    wv_pad = jnp.where(odd, jnp.concatenate([zv, vv], axis=2), jnp.concatenate([vv, zv], axis=2))
    hp = MLA_HEADS * LANES
    flat = lambda w: w.reshape(w.shape[0], hp).astype(BF16)
    return w_in_ext.astype(BF16), flat(wq_main), flat(wq_sw), flat(wk_pad), flat(wv_pad).T


def kernel(x, p, emb_ln_g, emb_ln_b, w_in, mla_q_norm_g, mla_kv_norm_g, w_uq, w_ukv, na_rpb, w_o, ln1_g, ln1_b,
           peer_w_q, peer_sub_keys, peer_u, peer_v, ple_w, ple_gate_w, ple_gate_b, ln2_g, ln2_b):
    b, seq, d = x.shape
    assert d == D_MODEL and seq % GRID_W == 0 and w_in.shape[0] == DEPTH
    t = b * seq
    vec = lambda a: a.reshape(1, -1).astype(F32)
    w_in_ext, wq_main, wq_sw, wk_pad, wv_pad_t = _prep_weights(w_in[0], w_uq[0], w_ukv[0])
    cos128, sin128 = _rope_tables(seq)

    h, qkv_na, q_mla, k_mla, v_mla_t = _proj(
        x.reshape(t, d), vec(emb_ln_g), vec(emb_ln_b), w_in_ext, vec(mla_q_norm_g[0]), vec(mla_kv_norm_g[0]),
        wq_main, wq_sw, wk_pad, wv_pad_t, cos128, sin128, seq, tm=PROJ_TM)

    bias = _na_bias(na_rpb[0].reshape(-1).astype(F32))
    a_na = _na_attn(qkv_na.reshape(b, seq, C_NA), bias, seq)
    hp = MLA_HEADS * LANES
    a_mla = _mla_attn(q_mla.reshape(b, seq, hp), k_mla.reshape(b, seq, hp), v_mla_t, tq=MLA_TQ)

    h1b, r2, e1, e2, gates = _mix(
        h, a_na.reshape(t, NA_WIDTH), a_mla.reshape(t, MLA_HEADS * MLA_V), p[0].reshape(t, PLE_DIM),
        w_o[0].astype(BF16), vec(ln1_g[0]), vec(ln1_b[0]), ple_gate_w[0].astype(BF16), vec(ple_gate_b[0]),
        ple_w[0].astype(BF16), peer_w_q[0].T.astype(BF16), peer_sub_keys[0].astype(BF16), tm=MIX_TM)

    out = _peer(h1b, e1, e2, gates, peer_u[0].astype(BF16), peer_v[0].astype(BF16), r2,
                vec(ln2_g[0]), vec(ln2_b[0]), tm=PEER_TM, eb=PEER_EB)
    return out.reshape(b, seq, d)
```

```python
import functools

import numpy as np
import jax
import jax.numpy as jnp
from jax import lax
from jax.experimental import pallas as pl
from jax.experimental.pallas import tpu as pltpu

F32 = jnp.float32
BF16 = jnp.bfloat16

D_MODEL = 1024
GRID_W = 64
NA_HEADS = 8
NA_HEAD_DIM = 64
NA_WIN_H = 8
NA_WIN_W = 16
NA_WIDTH = NA_HEADS * NA_HEAD_DIM
NA_BIAS_ROWS = 2 * NA_WIN_H - 1
NA_BIAS_COLS = 2 * NA_WIN_W - 1
MLA_HEADS = 8
MLA_Q_RANK = 384
MLA_KV_RANK = 256
MLA_NOPE = 64
MLA_ROPE = 32
MLA_V = 64
MLA_QD = MLA_NOPE + MLA_ROPE
ROPE_BASE = 10000.0
PEER_HEADS = 8
PEER_KEYS = 128
PEER_HALF = 128
PEER_TOPK = 16
PLE_DIM = 256
DEPTH = 1
DN_ALPHA = float((2 * DEPTH) ** 0.25)
LN_EPS = 1e-5
LANES = 128
NEG = -1e30
VMEM_LIMIT = 58 * 1024 * 1024

PROJ_TM = 512
MLA_TQ = 512
MIX_TM = 512
PEER_TM = 512
PEER_EB = 1024

C_NA = 3 * NA_WIDTH
C_CQ = C_NA + MLA_Q_RANK
C_CKV = C_CQ + MLA_KV_RANK
C_KRA = C_CKV + LANES
C_IN = C_KRA + LANES

NT = (((1,), (1,)), ((), ()))


def _layer_norm(x, g, b):
    mu = jnp.mean(x, axis=-1, keepdims=True)
    xc = x - mu
    var = jnp.mean(xc * xc, axis=-1, keepdims=True)
    return xc * lax.rsqrt(var + LN_EPS) * g + b


def _rms_norm(x, g):
    return x * lax.rsqrt(jnp.mean(x * x, axis=-1, keepdims=True) + LN_EPS) * g


def _mm(a, b):
    return jnp.dot(a, b, preferred_element_type=F32)


def _mm_nt(a, b):
    return lax.dot_general(a, b, NT, preferred_element_type=F32)


def _params(*sem):
    return pltpu.CompilerParams(dimension_semantics=sem, vmem_limit_bytes=VMEM_LIMIT)


def _full(shape):
    return pl.BlockSpec(shape, lambda *_: (0,) * len(shape))


def _proj_body(x_ref, g0_ref, b0_ref, win_ref, qg_ref, kvg_ref, wqm_ref, wqs_ref, wk_ref, wv_ref,
               cos_ref, sin_ref, h_ref, na_ref, q_ref, k_ref, vt_ref):
    h = _layer_norm(x_ref[...], g0_ref[...], b0_ref[...])
    h_ref[...] = h
    z = _mm(h.astype(BF16), win_ref[...])
    na_ref[:, :NA_WIDTH] = (z[:, :NA_WIDTH] * (NA_HEAD_DIM ** -0.5)).astype(BF16)
    na_ref[:, NA_WIDTH:] = z[:, NA_WIDTH:C_NA].astype(BF16)
    cqn = _rms_norm(z[:, C_NA:C_CQ], qg_ref[...]).astype(BF16)
    ckvn = _rms_norm(z[:, C_CQ:C_CKV], kvg_ref[...]).astype(BF16)
    cos = cos_ref[...]
    sin = sin_ref[...]
    q = _mm(cqn, wqm_ref[...]) * jnp.tile(cos, (1, MLA_HEADS)) + _mm(cqn, wqs_ref[...]) * jnp.tile(sin, (1, MLA_HEADS))
    q_ref[...] = (q * (MLA_QD ** -0.5)).astype(BF16)
    k_rot = z[:, C_CKV:C_KRA] * cos + z[:, C_KRA:C_IN] * sin
    k_ref[...] = (_mm(ckvn, wk_ref[...]) + jnp.tile(k_rot, (1, MLA_HEADS))).astype(BF16)
    vt_ref[...] = _mm_nt(wv_ref[...], ckvn).astype(BF16)


def _proj(x2, g0, b0, w_in_ext, qg, kvg, wq_main, wq_sw, wk_pad, wv_pad_t, cos128, sin128, seq, tm):
    t = x2.shape[0]
    n_pos = seq // tm
    row = lambda w: pl.BlockSpec((tm, w), lambda i: (i, 0))
    pos = pl.BlockSpec((tm, LANES), lambda i: (i % n_pos, 0))
    hp = MLA_HEADS * LANES
    return pl.pallas_call(
        _proj_body,
        grid=(t // tm,),
        in_specs=[row(D_MODEL), _full((1, D_MODEL)), _full((1, D_MODEL)), _full((D_MODEL, C_IN)),
                  _full((1, MLA_Q_RANK)), _full((1, MLA_KV_RANK)), _full((MLA_Q_RANK, hp)), _full((MLA_Q_RANK, hp)),
                  _full((MLA_KV_RANK, hp)), _full((hp, MLA_KV_RANK)), pos, pos],
        out_specs=[row(D_MODEL), row(C_NA), row(hp), row(hp), pl.BlockSpec((hp, tm), lambda i: (0, i))],
        out_shape=[jax.ShapeDtypeStruct((t, D_MODEL), F32), jax.ShapeDtypeStruct((t, C_NA), BF16),
                   jax.ShapeDtypeStruct((t, hp), BF16), jax.ShapeDtypeStruct((t, hp), BF16),
                   jax.ShapeDtypeStruct((hp, t), BF16)],
        compiler_params=_params("parallel"),
        name="proj",
    )(x2, g0, b0, w_in_ext, qg, kvg, wq_main, wq_sw, wk_pad, wv_pad_t, cos128, sin128)


def _na_bias_body(rpb_ref, o_ref):
    hh = pl.program_id(0)
    lane = lax.broadcasted_iota(jnp.int32, (GRID_W, LANES), 1)
    qc = lax.broadcasted_iota(jnp.int32, (GRID_W, LANES), 0)
    kc = lane % GRID_W
    first = lane < GRID_W
    dj = kc - qc + (NA_WIN_W - 1)
    cs = jnp.clip(qc - NA_WIN_W // 2, 0, GRID_W - NA_WIN_W)
    valid = (kc >= cs) & (kc < cs + NA_WIN_W)
    pair = []
    for a in range(NA_BIAS_ROWS - 1):
        acc = jnp.full((GRID_W, LANES), NEG, F32)
        for d in range(NA_BIAS_COLS):
            lo = rpb_ref[(hh * NA_BIAS_ROWS + a) * NA_BIAS_COLS + d]
            hi = rpb_ref[(hh * NA_BIAS_ROWS + a + 1) * NA_BIAS_COLS + d]
            acc = jnp.where(valid & (dj == d), jnp.where(first, lo, hi), acc)
        pair.append(acc)
    for d0 in range(NA_WIN_H):
        o_ref[0, d0] = jnp.concatenate([pair[d0 + 2 * i] for i in range(NA_WIN_H // 2)], axis=1)


def _na_bias(rpb_flat):
    band = NA_WIN_H * GRID_W
    return pl.pallas_call(
        _na_bias_body,
        grid=(NA_HEADS,),
        in_specs=[pl.BlockSpec(memory_space=pltpu.SMEM)],
        out_specs=pl.BlockSpec((1, NA_WIN_H, GRID_W, band), lambda h: (h, 0, 0, 0)),
        out_shape=jax.ShapeDtypeStruct((NA_HEADS, NA_WIN_H, GRID_W, band), F32),
        compiler_params=_params("arbitrary"),
        name="na_bias",
    )(rpb_flat)


NA_ROW_UNROLL = 2


def _na_body(q_ref, k_ref, v_ref, bias_ref, o_ref, *, rows):
    first = lax.broadcasted_iota(jnp.int32, (GRID_W, LANES), 1) < NA_HEAD_DIM
    band = NA_WIN_H * GRID_W

    def one_row(r, carry):
        rs = jnp.clip(r - NA_WIN_H // 2, 0, rows - NA_WIN_H)
        d0 = rs - r + (NA_WIN_H - 1)
        q0 = pl.multiple_of(r * GRID_W, GRID_W)
        k0 = pl.multiple_of(rs * GRID_W, GRID_W)
        pairs = range(NA_HEADS // 2)
        scores = []
        for j in pairs:
            cols = slice(j * LANES, (j + 1) * LANES)
            qp = q_ref[0, pl.ds(q0, GRID_W), cols]
            zero = jnp.zeros_like(qp)
            qs = jnp.concatenate([jnp.where(first, qp, zero), jnp.where(first, zero, qp)], axis=0)
            bias = jnp.concatenate([bias_ref[2 * j, d0], bias_ref[2 * j + 1, d0]], axis=0)
            scores.append(_mm_nt(qs, k_ref[0, pl.ds(k0, band), cols]) + bias)
        probs, sums = [], []
        for s in scores:
            p = jnp.exp(s - jnp.max(s, axis=-1, keepdims=True))
            sums.append(jnp.sum(p, axis=-1, keepdims=True))
            probs.append(p.astype(BF16))
        for j in pairs:
            cols = slice(j * LANES, (j + 1) * LANES)
            o = _mm(probs[j], v_ref[0, pl.ds(k0, band), cols]) / sums[j]
            o_ref[0, pl.ds(q0, GRID_W), cols] = jnp.where(first, o[:GRID_W], o[GRID_W:]).astype(BF16)
        return carry

    lax.fori_loop(0, rows, one_row, 0, unroll=NA_ROW_UNROLL)


def _na_attn(qkv, bias, seq):
    b = qkv.shape[0]
    rows = seq // GRID_W
    part = lambda c: pl.BlockSpec((1, seq, NA_WIDTH), lambda i: (i, 0, c))
    return pl.pallas_call(
        functools.partial(_na_body, rows=rows),
        grid=(b,),
        in_specs=[part(0), part(1), part(2), _full(bias.shape)],
        out_specs=pl.BlockSpec((1, seq, NA_WIDTH), lambda i: (i, 0, 0)),
        out_shape=jax.ShapeDtypeStruct((b, seq, NA_WIDTH), BF16),
        compiler_params=_params("parallel"),
        name="na_attn",
    )(qkv, qkv, qkv, bias)


def _mla_body(q_ref, k_ref, vt_ref, o_ref):
    def scores(j):
        out = []
        for e in range(2):
            cols = slice((2 * j + e) * LANES, (2 * j + e + 1) * LANES)
            out.append(_mm_nt(k_ref[0, :, cols], q_ref[0, :, cols]))
        return out

    pairs = MLA_HEADS // 2
    nxt = scores(0)
    for j in range(pairs):
        cur = nxt
        if j + 1 < pairs:
            nxt = scores(j + 1)
        acc = None
        for e in range(2):
            s = cur[e]
            p = jnp.exp(s - jnp.max(s, axis=0, keepdims=True))
            l = jnp.sum(p, axis=0, keepdims=True)
            rows = slice((2 * j + e) * LANES, (2 * j + e + 1) * LANES)
            o = _mm(vt_ref[rows, :], p.astype(BF16)) / l
            acc = o if acc is None else acc + o
        o_ref[0, :, j * LANES:(j + 1) * LANES] = acc.T.astype(BF16)


def _mla_attn(q, k, v_t, tq):
    b, seq, hp = q.shape
    width = MLA_HEADS * MLA_V
    return pl.pallas_call(
        _mla_body,
        grid=(b, seq // tq),
        in_specs=[pl.BlockSpec((1, tq, hp), lambda i, j: (i, j, 0)), pl.BlockSpec((1, seq, hp), lambda i, j: (i, 0, 0)),
                  pl.BlockSpec((hp, seq), lambda i, j: (0, i))],
        out_specs=pl.BlockSpec((1, tq, width), lambda i, j: (i, j, 0)),
        out_shape=jax.ShapeDtypeStruct((b, seq, width), BF16),
        compiler_params=_params("parallel", "arbitrary"),
        name="mla_attn",
    )(q, k, v_t)


SUBLANES = 8


COLUMN_DEPTH = 8


def _top16_columns(rows, tags):
    groups = []
    for g in range(0, len(rows), COLUMN_DEPTH):
        v, t = list(rows[g:g + COLUMN_DEPTH]), list(tags[g:g + COLUMN_DEPTH])
        for rnd in range(COLUMN_DEPTH):
            for a in range(rnd % 2, COLUMN_DEPTH - 1, 2):
                swap = v[a + 1] > v[a]
                v[a], v[a + 1] = jnp.maximum(v[a], v[a + 1]), jnp.minimum(v[a], v[a + 1])
                t[a], t[a + 1] = jnp.where(swap, t[a + 1], t[a]), jnp.where(swap, t[a], t[a + 1])
        groups.append((v, t))
    big = jnp.int32(1 << 20)
    vals, sels = [], []
    for extraction in range(PEER_TOPK):
        hv, ht = [v[0] for v, _ in groups], [t[0] for _, t in groups]
        while len(hv) > 1:
            nv, nt = [], []
            for a in range(0, len(hv) - 1, 2):
                keep = hv[a] >= hv[a + 1]
                nv.append(jnp.where(keep, hv[a], hv[a + 1]))
                nt.append(jnp.where(keep, ht[a], ht[a + 1]))
            if len(hv) % 2:
                nv.append(hv[-1])
                nt.append(ht[-1])
            hv, ht = nv, nt
        m = jnp.max(hv[0], axis=0, keepdims=True)
        sel = jnp.min(jnp.where(hv[0] == m, ht[0], big), axis=0, keepdims=True)
        vals.append(m)
        sels.append(sel)
        live = min(COLUMN_DEPTH, PEER_TOPK - extraction)
        for v, t in groups:
            pop = t[0] == sel
            for k in range(live - 1):
                v[k] = jnp.where(pop, v[k + 1], v[k])
                t[k] = jnp.where(pop, t[k + 1], t[k])
            if live == COLUMN_DEPTH:
                v[live - 1] = jnp.where(pop, -jnp.inf, v[live - 1])
    return jnp.concatenate(vals, axis=0), jnp.concatenate(sels, axis=0)


def _top16_presorted(groups):
    groups = [(list(v), list(t)) for v, t in groups]
    big = jnp.int32(1 << 20)
    vals, sels = [], []
    for extraction in range(PEER_TOPK):
        hv, ht = groups[0][0][0], groups[0][1][0]
        for v, t in groups[1:]:
            keep = (hv > v[0]) | ((hv == v[0]) & (ht < t[0]))
            hv, ht = jnp.where(keep, hv, v[0]), jnp.where(keep, ht, t[0])
        m = jnp.max(hv, axis=0, keepdims=True)
        sel = jnp.min(jnp.where(hv == m, ht, big), axis=0, keepdims=True)
        vals.append(m)
        sels.append(sel)
        for v, t in groups:
            live = min(len(v), PEER_TOPK - extraction)
            pop = t[0] == sel
            for k in range(live - 1):
                v[k] = jnp.where(pop, v[k + 1], v[k])
                t[k] = jnp.where(pop, t[k + 1], t[k])
            if live == len(v):
                v[live - 1] = jnp.where(pop, -jnp.inf, v[live - 1])
    return jnp.concatenate(vals, axis=0), jnp.concatenate(sels, axis=0)


def _tiles(x):
    return [x[r:r + SUBLANES, :] for r in range(0, x.shape[0], SUBLANES)]


_HALF_RANKS = PEER_TOPK // 2


def _head_topk(s1, s2):
    tm = s1.shape[1]
    r8 = lax.broadcasted_iota(jnp.int32, (_HALF_RANKS, tm), 0)
    key_tags = [r8 + SUBLANES * v for v in range(PEER_KEYS // SUBLANES)]
    v1, i1 = _top16_columns(_tiles(s1), key_tags)
    v2, i2 = _top16_columns(_tiles(s2), key_tags)
    low = v2[:_HALF_RANKS, :]
    grid_v = [v1[i:i + 1, :] + low for i in range(PEER_TOPK)]
    grid_t = [i * PEER_TOPK + r8 for i in range(PEER_TOPK)]
    cv, ci = _top16_presorted([(grid_v, grid_t), ([v1[0:1, :] + v2[_HALF_RANKS:, :]], [r8 + _HALF_RANKS])])
    ci1 = lax.shift_right_logical(ci, 4)
    ci2 = lax.bitwise_and(ci, PEER_TOPK - 1)
    e1 = jnp.zeros((PEER_TOPK, tm), jnp.int32)
    e2 = jnp.zeros((PEER_TOPK, tm), jnp.int32)
    for r in range(PEER_TOPK):
        e1 = jnp.where(ci1 == r, i1[r:r + 1, :], e1)
        e2 = jnp.where(ci2 == r, i2[r:r + 1, :], e2)
    p = jnp.exp(cv - jnp.max(cv, axis=0, keepdims=True))
    return e1, e2, p / jnp.sum(p, axis=0, keepdims=True)


ROUTE_LANES = 256


def _mix_body(h_ref, na_ref, mla_ref, p_ref, wo_ref, g1_ref, b1_ref, wg_ref, bg_ref, wple_ref, wqt_ref, keys_ref,
              h1b_ref, r2_ref, e1_ref, e2_ref, g_ref):
    tm = h_ref.shape[0]
    mix = _mm(na_ref[...], wo_ref[:NA_WIDTH, :]) + _mm(mla_ref[...], wo_ref[NA_WIDTH:, :])
    h1 = _layer_norm(DN_ALPHA * h_ref[...] + mix, g1_ref[...], b1_ref[...])
    h1b = h1.astype(BF16)
    h1b_ref[...] = h1b
    gate = jax.nn.sigmoid(_mm(h1b, wg_ref[...]) + bg_ref[...])
    ple = gate * _mm(p_ref[...].astype(BF16), wple_ref[...])
    r2_ref[...] = DN_ALPHA * h1 + ple
    q_t = _mm_nt(wqt_ref[...], h1b).astype(BF16)
    e1s, e2s, gs = [], [], []
    for hh in range(PEER_HEADS):
        base = hh * 2 * PEER_HALF
        s1 = _mm(keys_ref[0], q_t[base:base + PEER_HALF, :])
        s2 = _mm(keys_ref[1], q_t[base + PEER_HALF:base + 2 * PEER_HALF, :])
        parts = [_head_topk(s1[:, c:c + ROUTE_LANES], s2[:, c:c + ROUTE_LANES]) for c in range(0, tm, ROUTE_LANES)]
        e1s.append(jnp.concatenate([p[0] for p in parts], axis=1))
        e2s.append(jnp.concatenate([p[1] for p in parts], axis=1))
        gs.append(jnp.concatenate([p[2] for p in parts], axis=1))
    e1_ref[...] = jnp.concatenate(e1s, axis=0).T
    e2_ref[...] = jnp.concatenate(e2s, axis=0).T
    g_ref[...] = jnp.concatenate(gs, axis=0).T


def _mix(h, a_na, a_mla, p2, w_o, g1, b1, w_g, b_g, w_ple, w_qt, keys, tm):
    t = h.shape[0]
    row = lambda w: pl.BlockSpec((tm, w), lambda i: (i, 0))
    nq = 2 * PEER_HEADS * PEER_HALF
    npair = PEER_HEADS * PEER_TOPK
    return pl.pallas_call(
        _mix_body,
        grid=(t // tm,),
        in_specs=[row(D_MODEL), row(NA_WIDTH), row(MLA_HEADS * MLA_V), row(PLE_DIM), _full((D_MODEL, D_MODEL)),
                  _full((1, D_MODEL)), _full((1, D_MODEL)), _full((D_MODEL, D_MODEL)), _full((1, D_MODEL)),
                  _full((PLE_DIM, D_MODEL)), _full((nq, D_MODEL)), _full((2, PEER_KEYS, PEER_HALF))],
        out_specs=[row(D_MODEL), row(D_MODEL), row(npair), row(npair), row(npair)],
        out_shape=[jax.ShapeDtypeStruct((t, D_MODEL), BF16), jax.ShapeDtypeStruct((t, D_MODEL), F32),
                   jax.ShapeDtypeStruct((t, npair), jnp.int32), jax.ShapeDtypeStruct((t, npair), jnp.int32),
                   jax.ShapeDtypeStruct((t, npair), F32)],
        compiler_params=_params("parallel"),
        name="mix",
    )(h, a_na, a_mla, p2, w_o, g1, b1, w_g, b_g, w_ple, w_qt, keys)


PEER_CHUNK = 64
WALL_PITCH = PEER_KEYS + 8


def _peer_body(x_ref, e1_ref, e2_ref, g_ref, u_ref, v_ref, r2_ref, g2_ref, b2_ref, o_ref, wall_ref):
    j = pl.program_id(1)
    tm = x_ref.shape[0]
    eb = u_ref.shape[0]
    npair = e1_ref.shape[1]

    @pl.when(j == 0)
    def _():
        o_ref[...] = jnp.zeros_like(o_ref)
        key_id = lax.broadcasted_iota(jnp.int32, (PEER_CHUNK, PEER_KEYS, 2 * npair), 1)

        def chunk(c, carry):
            rows = pl.ds(pl.multiple_of(c * PEER_CHUNK, PEER_CHUNK), PEER_CHUNK)
            e1 = e1_ref[rows, :]
            e2 = e2_ref[rows, :]
            g = g_ref[rows, :]
            g_hi = g.astype(BF16).astype(F32)
            gx = jnp.concatenate([g_hi, g - g_hi], axis=1)[:, None, :]
            e1x = jnp.concatenate([e1, e1], axis=1)[:, None, :]
            e2x = jnp.concatenate([e2, e2], axis=1)[:, None, :]
            a = jnp.where(e1x == key_id, 1.0, 0.0).astype(BF16)
            bm = jnp.where(e2x == key_id, gx, 0.0).astype(BF16)
            w = jnp.einsum("tep,tfp->tef", a, bm, preferred_element_type=F32)
            for i in range(PEER_CHUNK):
                base = pl.multiple_of((c * PEER_CHUNK + i) * WALL_PITCH, 8)
                wall_ref[pl.ds(base, PEER_KEYS), :] = w[i]
            return carry

        lax.fori_loop(0, tm // PEER_CHUNK, chunk, 0)

    nsub = eb // PEER_KEYS
    hpre = _mm_nt(x_ref[...], u_ref[...])
    w = jnp.concatenate([wall_ref[pl.ds(j * nsub + c, tm, stride=WALL_PITCH), :] for c in range(nsub)], axis=1)
    act = w * (0.5 * hpre * (1.0 + lax.erf(hpre * (2.0 ** -0.5))))
    o_ref[...] += _mm(act.astype(BF16), v_ref[...])

    @pl.when(j == pl.num_programs(1) - 1)
    def _():
        o_ref[...] = _layer_norm(r2_ref[...] + o_ref[...], g2_ref[...], b2_ref[...])


def _peer(h1b, e1, e2, g, u, v, r2, g2, b2, tm, eb):
    t = h1b.shape[0]
    n_exp = u.shape[0]
    npair = e1.shape[1]
    row = lambda w: pl.BlockSpec((tm, w), lambda i, j: (i, 0))
    tab = pl.BlockSpec((eb, D_MODEL), lambda i, j: (j, 0))
    vec = pl.BlockSpec((1, D_MODEL), lambda i, j: (0, 0))
    resid = pl.BlockSpec((tm, D_MODEL), lambda i, j: (i, 0), pipeline_mode=pl.Buffered(1))
    return pl.pallas_call(
        _peer_body,
        grid=(t // tm, n_exp // eb),
        in_specs=[row(D_MODEL), row(npair), row(npair), row(npair), tab, tab, resid, vec, vec],
        out_specs=row(D_MODEL),
        out_shape=jax.ShapeDtypeStruct((t, D_MODEL), F32),
        scratch_shapes=[pltpu.VMEM((tm * WALL_PITCH, PEER_KEYS), F32)],
        compiler_params=_params("parallel", "arbitrary"),
        name="peer",
    )(h1b, e1, e2, g, u, v, r2, g2, b2)


def _rope_tables(seq):
    t = jnp.arange(seq)
    row = (t // GRID_W).astype(F32)
    col = (t % GRID_W).astype(F32)
    axis_dim = MLA_ROPE // 2
    inv = ROPE_BASE ** (-jnp.arange(0, axis_dim, 2, dtype=F32) / axis_dim)
    ang = jnp.concatenate([row[:, None] * inv[None, :], col[:, None] * inv[None, :]], axis=-1)
    cos, sin = jnp.cos(ang), jnp.sin(ang)
    pad = LANES - MLA_NOPE - MLA_ROPE
    cos128 = jnp.concatenate([jnp.ones((seq, MLA_NOPE), F32), cos, cos, jnp.zeros((seq, pad), F32)], axis=1)
    sin128 = jnp.concatenate([jnp.zeros((seq, MLA_NOPE), F32), -sin, sin, jnp.zeros((seq, pad), F32)], axis=1)
    return cos128, sin128


_PERM = np.concatenate([np.arange(0, MLA_ROPE, 2), np.arange(1, MLA_ROPE, 2)])
_PERM_SW = np.concatenate([np.arange(1, MLA_ROPE, 2), np.arange(0, MLA_ROPE, 2)])


def _prep_weights(w_in, w_uq, w_ukv):
    pad = LANES - MLA_NOPE - MLA_ROPE
    zin = lambda n: jnp.zeros((D_MODEL, n), F32)
    kr = w_in[:, C_CKV:]
    w_in_ext = jnp.concatenate(
        [w_in[:, :C_CKV], zin(MLA_NOPE), kr[:, _PERM], zin(pad), zin(MLA_NOPE), kr[:, _PERM_SW], zin(pad)], axis=1)
    wq = w_uq.reshape(MLA_Q_RANK, MLA_HEADS, MLA_QD)
    zq = lambda n: jnp.zeros((MLA_Q_RANK, MLA_HEADS, n), F32)
    rope = wq[:, :, MLA_NOPE:]
    wq_main = jnp.concatenate([wq[:, :, :MLA_NOPE], rope[:, :, _PERM], zq(pad)], axis=2)
    wq_sw = jnp.concatenate([zq(MLA_NOPE), rope[:, :, _PERM_SW], zq(pad)], axis=2)
    wkv = w_ukv.reshape(MLA_KV_RANK, MLA_HEADS, MLA_NOPE + MLA_V)
    zk = jnp.zeros((MLA_KV_RANK, MLA_HEADS, LANES - MLA_NOPE), F32)
    wk_pad = jnp.concatenate([wkv[:, :, :MLA_NOPE], zk], axis=2)
    vv = wkv[:, :, MLA_NOPE:]
    zv = jnp.zeros_like(vv)
    odd = (jnp.arange(MLA_HEADS) % 2 == 1)[None, :, None]
    wv_pad = jnp.where(odd, jnp.concatenate([zv, vv], axis=2), jnp.concatenate([vv, zv], axis=2))
    hp = MLA_HEADS * LANES
    flat = lambda w: w.reshape(w.shape[0], hp).astype(BF16)
    return w_in_ext.astype(BF16), flat(wq_main), flat(wq_sw), flat(wk_pad), flat(wv_pad).T


def kernel(x, p, emb_ln_g, emb_ln_b, w_in, mla_q_norm_g, mla_kv_norm_g, w_uq, w_ukv, na_rpb, w_o, ln1_g, ln1_b,
           peer_w_q, peer_sub_keys, peer_u, peer_v, ple_w, ple_gate_w, ple_gate_b, ln2_g, ln2_b):
    b, seq, d = x.shape
    assert d == D_MODEL and seq % GRID_W == 0 and w_in.shape[0] == DEPTH
    t = b * seq
    vec = lambda a: a.reshape(1, -1).astype(F32)
    w_in_ext, wq_main, wq_sw, wk_pad, wv_pad_t = _prep_weights(w_in[0], w_uq[0], w_ukv[0])
    cos128, sin128 = _rope_tables(seq)

    h, qkv_na, q_mla, k_mla, v_mla_t = _proj(
        x.reshape(t, d), vec(emb_ln_g), vec(emb_ln_b), w_in_ext, vec(mla_q_norm_g[0]), vec(mla_kv_norm_g[0]),
        wq_main, wq_sw, wk_pad, wv_pad_t, cos128, sin128, seq, tm=PROJ_TM)

    bias = _na_bias(na_rpb[0].reshape(-1).astype(F32))
    a_na = _na_attn(qkv_na.reshape(b, seq, C_NA), bias, seq)
    hp = MLA_HEADS * LANES
    a_mla = _mla_attn(q_mla.reshape(b, seq, hp), k_mla.reshape(b, seq, hp), v_mla_t, tq=MLA_TQ)

    h1b, r2, e1, e2, gates = _mix(
        h, a_na.reshape(t, NA_WIDTH), a_mla.reshape(t, MLA_HEADS * MLA_V), p[0].reshape(t, PLE_DIM),
        w_o[0].astype(BF16), vec(ln1_g[0]), vec(ln1_b[0]), ple_gate_w[0].astype(BF16), vec(ple_gate_b[0]),
        ple_w[0].astype(BF16), peer_w_q[0].T.astype(BF16), peer_sub_keys[0].astype(BF16), tm=MIX_TM)

    out = _peer(h1b, e1, e2, gates, peer_u[0].astype(BF16), peer_v[0].astype(BF16), r2,
                vec(ln2_g[0]), vec(ln2_b[0]), tm=PEER_TM, eb=PEER_EB)
    return out.reshape(b, seq, d)
```

```python
import functools

import numpy as np
import jax
import jax.numpy as jnp
from jax import lax
from jax.experimental import pallas as pl
from jax.experimental.pallas import tpu as pltpu

F32 = jnp.float32
BF16 = jnp.bfloat16

D_MODEL = 1024
GRID_W = 64
NA_HEADS = 8
NA_HEAD_DIM = 64
NA_WIN_H = 8
NA_WIN_W = 16
NA_WIDTH = NA_HEADS * NA_HEAD_DIM
NA_BIAS_ROWS = 2 * NA_WIN_H - 1
NA_BIAS_COLS = 2 * NA_WIN_W - 1
MLA_HEADS = 8
MLA_Q_RANK = 384
MLA_KV_RANK = 256
MLA_NOPE = 64
MLA_ROPE = 32
MLA_V = 64
MLA_QD = MLA_NOPE + MLA_ROPE
ROPE_BASE = 10000.0
PEER_HEADS = 8
PEER_KEYS = 128
PEER_HALF = 128
PEER_TOPK = 16
PLE_DIM = 256
DEPTH = 1
DN_ALPHA = float((2 * DEPTH) ** 0.25)
LN_EPS = 1e-5
LOG2E = 1.4426950408889634
LANES = 128
NEG = -1e30
VMEM_LIMIT = 58 * 1024 * 1024

PROJ_TM = 512
MLA_TQ = 512
MIX_TM = 512
PEER_TM = 512
PEER_EB = 1024

C_NA = 3 * NA_WIDTH
C_CQ = C_NA + MLA_Q_RANK
C_CKV = C_CQ + MLA_KV_RANK
C_KRA = C_CKV + LANES
C_IN = C_KRA + LANES

NT = (((1,), (1,)), ((), ()))


def _layer_norm(x, g, b):
    mu = jnp.mean(x, axis=-1, keepdims=True)
    xc = x - mu
    var = jnp.mean(xc * xc, axis=-1, keepdims=True)
    return xc * lax.rsqrt(var + LN_EPS) * g + b


def _rms_norm(x, g):
    return x * lax.rsqrt(jnp.mean(x * x, axis=-1, keepdims=True) + LN_EPS) * g


def _mm(a, b):
    return jnp.dot(a, b, preferred_element_type=F32)


def _mm_nt(a, b):
    return lax.dot_general(a, b, NT, preferred_element_type=F32)


def _params(*sem):
    return pltpu.CompilerParams(dimension_semantics=sem, vmem_limit_bytes=VMEM_LIMIT)


def _full(shape):
    return pl.BlockSpec(shape, lambda *_: (0,) * len(shape))


def _proj_body(x_ref, g0_ref, b0_ref, win_ref, qg_ref, kvg_ref, wqm_ref, wqs_ref, wk_ref, wv_ref,
               cos_ref, sin_ref, h_ref, na_ref, q_ref, k_ref, vt_ref):
    h = _layer_norm(x_ref[...], g0_ref[...], b0_ref[...])
    h_ref[...] = h
    z = _mm(h.astype(BF16), win_ref[...])
    na_ref[:, :NA_WIDTH] = (z[:, :NA_WIDTH] * (NA_HEAD_DIM ** -0.5)).astype(BF16)
    na_ref[:, NA_WIDTH:] = z[:, NA_WIDTH:C_NA].astype(BF16)
    cqn = _rms_norm(z[:, C_NA:C_CQ], qg_ref[...]).astype(BF16)
    ckvn = _rms_norm(z[:, C_CQ:C_CKV], kvg_ref[...]).astype(BF16)
    cos = cos_ref[...]
    sin = sin_ref[...]
    q = _mm(cqn, wqm_ref[...]) * jnp.tile(cos, (1, MLA_HEADS)) + _mm(cqn, wqs_ref[...]) * jnp.tile(sin, (1, MLA_HEADS))
    q_ref[...] = (q * (MLA_QD ** -0.5 * LOG2E)).astype(BF16)
    k_rot = z[:, C_CKV:C_KRA] * cos + z[:, C_KRA:C_IN] * sin
    k_ref[...] = (_mm(ckvn, wk_ref[...]) + jnp.tile(k_rot, (1, MLA_HEADS))).astype(BF16)
    vt_ref[...] = _mm_nt(wv_ref[...], ckvn).astype(BF16)


def _proj(x2, g0, b0, w_in_ext, qg, kvg, wq_main, wq_sw, wk_pad, wv_pad_t, cos128, sin128, seq, tm):
    t = x2.shape[0]
    n_pos = seq // tm
    row = lambda w: pl.BlockSpec((tm, w), lambda i: (i, 0))
    pos = pl.BlockSpec((tm, LANES), lambda i: (i % n_pos, 0))
    hp = MLA_HEADS * LANES
    return pl.pallas_call(
        _proj_body,
        grid=(t // tm,),
        in_specs=[row(D_MODEL), _full((1, D_MODEL)), _full((1, D_MODEL)), _full((D_MODEL, C_IN)),
                  _full((1, MLA_Q_RANK)), _full((1, MLA_KV_RANK)), _full((MLA_Q_RANK, hp)), _full((MLA_Q_RANK, hp)),
                  _full((MLA_KV_RANK, hp)), _full((hp, MLA_KV_RANK)), pos, pos],
        out_specs=[row(D_MODEL), row(C_NA), row(hp), row(hp), pl.BlockSpec((hp, tm), lambda i: (0, i))],
        out_shape=[jax.ShapeDtypeStruct((t, D_MODEL), F32), jax.ShapeDtypeStruct((t, C_NA), BF16),
                   jax.ShapeDtypeStruct((t, hp), BF16), jax.ShapeDtypeStruct((t, hp), BF16),
                   jax.ShapeDtypeStruct((hp, t), BF16)],
        compiler_params=_params("parallel"),
        name="proj",
    )(x2, g0, b0, w_in_ext, qg, kvg, wq_main, wq_sw, wk_pad, wv_pad_t, cos128, sin128)


def _na_bias_body(rpb_ref, o_ref):
    hh = pl.program_id(0)
    lane = lax.broadcasted_iota(jnp.int32, (GRID_W, LANES), 1)
    qc = lax.broadcasted_iota(jnp.int32, (GRID_W, LANES), 0)
    kc = lane % GRID_W
    first = lane < GRID_W
    dj = kc - qc + (NA_WIN_W - 1)
    cs = jnp.clip(qc - NA_WIN_W // 2, 0, GRID_W - NA_WIN_W)
    valid = (kc >= cs) & (kc < cs + NA_WIN_W)
    pair = []
    for a in range(NA_BIAS_ROWS - 1):
        acc = jnp.full((GRID_W, LANES), NEG, F32)
        for d in range(NA_BIAS_COLS):
            lo = rpb_ref[(hh * NA_BIAS_ROWS + a) * NA_BIAS_COLS + d]
            hi = rpb_ref[(hh * NA_BIAS_ROWS + a + 1) * NA_BIAS_COLS + d]
            acc = jnp.where(valid & (dj == d), jnp.where(first, lo, hi), acc)
        pair.append(acc)
    for d0 in range(NA_WIN_H):
        o_ref[0, d0] = jnp.concatenate([pair[d0 + 2 * i] for i in range(NA_WIN_H // 2)], axis=1)


def _na_bias(rpb_flat):
    band = NA_WIN_H * GRID_W
    return pl.pallas_call(
        _na_bias_body,
        grid=(NA_HEADS,),
        in_specs=[pl.BlockSpec(memory_space=pltpu.SMEM)],
        out_specs=pl.BlockSpec((1, NA_WIN_H, GRID_W, band), lambda h: (h, 0, 0, 0)),
        out_shape=jax.ShapeDtypeStruct((NA_HEADS, NA_WIN_H, GRID_W, band), F32),
        compiler_params=_params("arbitrary"),
        name="na_bias",
    )(rpb_flat)


NA_ROW_UNROLL = 2


def _na_body(q_ref, k_ref, v_ref, bias_ref, o_ref, *, rows):
    first = lax.broadcasted_iota(jnp.int32, (GRID_W, LANES), 1) < NA_HEAD_DIM
    band = NA_WIN_H * GRID_W

    def one_row(r, carry):
        rs = jnp.clip(r - NA_WIN_H // 2, 0, rows - NA_WIN_H)
        d0 = rs - r + (NA_WIN_H - 1)
        q0 = pl.multiple_of(r * GRID_W, GRID_W)
        k0 = pl.multiple_of(rs * GRID_W, GRID_W)
        pairs = range(NA_HEADS // 2)
        scores = []
        for j in pairs:
            cols = slice(j * LANES, (j + 1) * LANES)
            qp = q_ref[0, pl.ds(q0, GRID_W), cols]
            zero = jnp.zeros_like(qp)
            qs = jnp.concatenate([jnp.where(first, qp, zero), jnp.where(first, zero, qp)], axis=0)
            bias = jnp.concatenate([bias_ref[2 * j, d0], bias_ref[2 * j + 1, d0]], axis=0)
            scores.append(_mm_nt(qs, k_ref[0, pl.ds(k0, band), cols]) + bias)
        probs, sums = [], []
        for s in scores:
            p = jnp.exp(s - jnp.max(s, axis=-1, keepdims=True))
            sums.append(jnp.sum(p, axis=-1, keepdims=True))
            probs.append(p.astype(BF16))
        for j in pairs:
            cols = slice(j * LANES, (j + 1) * LANES)
            o = _mm(probs[j], v_ref[0, pl.ds(k0, band), cols]) / sums[j]
            o_ref[0, pl.ds(q0, GRID_W), cols] = jnp.where(first, o[:GRID_W], o[GRID_W:]).astype(BF16)
        return carry

    lax.fori_loop(0, rows, one_row, 0, unroll=NA_ROW_UNROLL)


def _na_attn(qkv, bias, seq):
    b = qkv.shape[0]
    rows = seq // GRID_W
    part = lambda c: pl.BlockSpec((1, seq, NA_WIDTH), lambda i: (i, 0, c))
    return pl.pallas_call(
        functools.partial(_na_body, rows=rows),
        grid=(b,),
        in_specs=[part(0), part(1), part(2), _full(bias.shape)],
        out_specs=pl.BlockSpec((1, seq, NA_WIDTH), lambda i: (i, 0, 0)),
        out_shape=jax.ShapeDtypeStruct((b, seq, NA_WIDTH), BF16),
        compiler_params=_params("parallel"),
        name="na_attn",
    )(qkv, qkv, qkv, bias)


def _mla_body(q_ref, k_ref, vt_ref, o_ref):
    def scores(j):
        out = []
        for e in range(2):
            cols = slice((2 * j + e) * LANES, (2 * j + e + 1) * LANES)
            out.append(_mm_nt(k_ref[0, :, cols], q_ref[0, :, cols]))
        return out

    pairs = MLA_HEADS // 2
    nxt = scores(0)
    for j in range(pairs):
        cur = nxt
        if j + 1 < pairs:
            nxt = scores(j + 1)
        acc = None
        for e in range(2):
            s = cur[e]
            p = jnp.exp2(s - jnp.max(s, axis=0, keepdims=True))
            l = jnp.sum(p, axis=0, keepdims=True)
            rows = slice((2 * j + e) * LANES, (2 * j + e + 1) * LANES)
            o = _mm(vt_ref[rows, :], p.astype(BF16)) / l
            acc = o if acc is None else acc + o
        o_ref[0, :, j * LANES:(j + 1) * LANES] = acc.T.astype(BF16)


def _mla_attn(q, k, v_t, tq):
    b, seq, hp = q.shape
    width = MLA_HEADS * MLA_V
    return pl.pallas_call(
        _mla_body,
        grid=(b, seq // tq),
        in_specs=[pl.BlockSpec((1, tq, hp), lambda i, j: (i, j, 0)), pl.BlockSpec((1, seq, hp), lambda i, j: (i, 0, 0)),
                  pl.BlockSpec((hp, seq), lambda i, j: (0, i))],
        out_specs=pl.BlockSpec((1, tq, width), lambda i, j: (i, j, 0)),
        out_shape=jax.ShapeDtypeStruct((b, seq, width), BF16),
        compiler_params=_params("parallel", "arbitrary"),
        name="mla_attn",
    )(q, k, v_t)


SUBLANES = 8


COLUMN_DEPTH = 8


def _top16_columns(rows, tags):
    groups = []
    for g in range(0, len(rows), COLUMN_DEPTH):
        v, t = list(rows[g:g + COLUMN_DEPTH]), list(tags[g:g + COLUMN_DEPTH])
        for rnd in range(COLUMN_DEPTH):
            for a in range(rnd % 2, COLUMN_DEPTH - 1, 2):
                swap = v[a + 1] > v[a]
                v[a], v[a + 1] = jnp.maximum(v[a], v[a + 1]), jnp.minimum(v[a], v[a + 1])
                t[a], t[a + 1] = jnp.where(swap, t[a + 1], t[a]), jnp.where(swap, t[a], t[a + 1])
        groups.append((v, t))
    big = jnp.int32(1 << 20)
    vals, sels = [], []
    for extraction in range(PEER_TOPK):
        hv, ht = [v[0] for v, _ in groups], [t[0] for _, t in groups]
        while len(hv) > 1:
            nv, nt = [], []
            for a in range(0, len(hv) - 1, 2):
                keep = hv[a] >= hv[a + 1]
                nv.append(jnp.where(keep, hv[a], hv[a + 1]))
                nt.append(jnp.where(keep, ht[a], ht[a + 1]))
            if len(hv) % 2:
                nv.append(hv[-1])
                nt.append(ht[-1])
            hv, ht = nv, nt
        m = jnp.max(hv[0], axis=0, keepdims=True)
        sel = jnp.min(jnp.where(hv[0] == m, ht[0], big), axis=0, keepdims=True)
        vals.append(m)
        sels.append(sel)
        live = min(COLUMN_DEPTH, PEER_TOPK - extraction)
        for v, t in groups:
            pop = t[0] == sel
            for k in range(live - 1):
                v[k] = jnp.where(pop, v[k + 1], v[k])
                t[k] = jnp.where(pop, t[k + 1], t[k])
            if live == COLUMN_DEPTH:
                v[live - 1] = jnp.where(pop, -jnp.inf, v[live - 1])
    return jnp.concatenate(vals, axis=0), jnp.concatenate(sels, axis=0)


def _top16_presorted(groups):
    groups = [(list(v), list(t)) for v, t in groups]
    big = jnp.int32(1 << 20)
    vals, sels = [], []
    for extraction in range(PEER_TOPK):
        hv, ht = groups[0][0][0], groups[0][1][0]
        for v, t in groups[1:]:
            keep = (hv > v[0]) | ((hv == v[0]) & (ht < t[0]))
            hv, ht = jnp.where(keep, hv, v[0]), jnp.where(keep, ht, t[0])
        m = jnp.max(hv, axis=0, keepdims=True)
        sel = jnp.min(jnp.where(hv == m, ht, big), axis=0, keepdims=True)
        vals.append(m)
        sels.append(sel)
        for v, t in groups:
            live = min(len(v), PEER_TOPK - extraction)
            pop = t[0] == sel
            for k in range(live - 1):
                v[k] = jnp.where(pop, v[k + 1], v[k])
                t[k] = jnp.where(pop, t[k + 1], t[k])
            if live == len(v):
                v[live - 1] = jnp.where(pop, -jnp.inf, v[live - 1])
    return jnp.concatenate(vals, axis=0), jnp.concatenate(sels, axis=0)


def _tiles(x):
    return [x[r:r + SUBLANES, :] for r in range(0, x.shape[0], SUBLANES)]


_HALF_RANKS = PEER_TOPK // 2


def _head_topk(s1, s2):
    tm = s1.shape[1]
    r8 = lax.broadcasted_iota(jnp.int32, (_HALF_RANKS, tm), 0)
    key_tags = [r8 + SUBLANES * v for v in range(PEER_KEYS // SUBLANES)]
    v1, i1 = _top16_columns(_tiles(s1), key_tags)
    v2, i2 = _top16_columns(_tiles(s2), key_tags)
    low = v2[:_HALF_RANKS, :]
    grid_v = [v1[i:i + 1, :] + low for i in range(PEER_TOPK)]
    grid_t = [i * PEER_TOPK + r8 for i in range(PEER_TOPK)]
    cv, ci = _top16_presorted([(grid_v, grid_t), ([v1[0:1, :] + v2[_HALF_RANKS:, :]], [r8 + _HALF_RANKS])])
    ci1 = lax.shift_right_logical(ci, 4)
    ci2 = lax.bitwise_and(ci, PEER_TOPK - 1)
    e1 = jnp.zeros((PEER_TOPK, tm), jnp.int32)
    e2 = jnp.zeros((PEER_TOPK, tm), jnp.int32)
    for r in range(PEER_TOPK):
        e1 = jnp.where(ci1 == r, i1[r:r + 1, :], e1)
        e2 = jnp.where(ci2 == r, i2[r:r + 1, :], e2)
    p = jnp.exp(cv - jnp.max(cv, axis=0, keepdims=True))
    return e1, e2, p / jnp.sum(p, axis=0, keepdims=True)


ROUTE_LANES = 256


def _mix_body(h_ref, na_ref, mla_ref, p_ref, wo_ref, g1_ref, b1_ref, wg_ref, bg_ref, wple_ref, wqt_ref, keys_ref,
              h1b_ref, r2_ref, e1_ref, e2_ref, g_ref):
    tm = h_ref.shape[0]
    mix = _mm(na_ref[...], wo_ref[:NA_WIDTH, :]) + _mm(mla_ref[...], wo_ref[NA_WIDTH:, :])
    h1 = _layer_norm(DN_ALPHA * h_ref[...] + mix, g1_ref[...], b1_ref[...])
    h1b = h1.astype(BF16)
    h1b_ref[...] = h1b
    gate = jax.nn.sigmoid(_mm(h1b, wg_ref[...]) + bg_ref[...])
    ple = gate * _mm(p_ref[...].astype(BF16), wple_ref[...])
    r2_ref[...] = DN_ALPHA * h1 + ple
    q_t = _mm_nt(wqt_ref[...], h1b).astype(BF16)
    e1s, e2s, gs = [], [], []
    for hh in range(PEER_HEADS):
        base = hh * 2 * PEER_HALF
        s1 = _mm(keys_ref[0], q_t[base:base + PEER_HALF, :])
        s2 = _mm(keys_ref[1], q_t[base + PEER_HALF:base + 2 * PEER_HALF, :])
        parts = [_head_topk(s1[:, c:c + ROUTE_LANES], s2[:, c:c + ROUTE_LANES]) for c in range(0, tm, ROUTE_LANES)]
        e1s.append(jnp.concatenate([p[0] for p in parts], axis=1))
        e2s.append(jnp.concatenate([p[1] for p in parts], axis=1))
        gs.append(jnp.concatenate([p[2] for p in parts], axis=1))
    e1_ref[...] = jnp.concatenate(e1s, axis=0).T
    e2_ref[...] = jnp.concatenate(e2s, axis=0).T
    g_ref[...] = jnp.concatenate(gs, axis=0).T


def _mix(h, a_na, a_mla, p2, w_o, g1, b1, w_g, b_g, w_ple, w_qt, keys, tm):
    t = h.shape[0]
    row = lambda w: pl.BlockSpec((tm, w), lambda i: (i, 0))
    nq = 2 * PEER_HEADS * PEER_HALF
    npair = PEER_HEADS * PEER_TOPK
    return pl.pallas_call(
        _mix_body,
        grid=(t // tm,),
        in_specs=[row(D_MODEL), row(NA_WIDTH), row(MLA_HEADS * MLA_V), row(PLE_DIM), _full((D_MODEL, D_MODEL)),
                  _full((1, D_MODEL)), _full((1, D_MODEL)), _full((D_MODEL, D_MODEL)), _full((1, D_MODEL)),
                  _full((PLE_DIM, D_MODEL)), _full((nq, D_MODEL)), _full((2, PEER_KEYS, PEER_HALF))],
        out_specs=[row(D_MODEL), row(D_MODEL), row(npair), row(npair), row(npair)],
        out_shape=[jax.ShapeDtypeStruct((t, D_MODEL), BF16), jax.ShapeDtypeStruct((t, D_MODEL), F32),
                   jax.ShapeDtypeStruct((t, npair), jnp.int32), jax.ShapeDtypeStruct((t, npair), jnp.int32),
                   jax.ShapeDtypeStruct((t, npair), F32)],
        compiler_params=_params("parallel"),
        name="mix",
    )(h, a_na, a_mla, p2, w_o, g1, b1, w_g, b_g, w_ple, w_qt, keys)


PEER_CHUNK = 64
WALL_PITCH = PEER_KEYS + 8


def _peer_body(x_ref, e1_ref, e2_ref, g_ref, u_ref, v_ref, r2_ref, g2_ref, b2_ref, o_ref, wall_ref):
    j = pl.program_id(1)
    tm = x_ref.shape[0]
    eb = u_ref.shape[0]
    npair = e1_ref.shape[1]

    @pl.when(j == 0)
    def _():
        o_ref[...] = jnp.zeros_like(o_ref)
        key_id = lax.broadcasted_iota(jnp.int32, (PEER_CHUNK, PEER_KEYS, 2 * npair), 1)

        def chunk(c, carry):
            rows = pl.ds(pl.multiple_of(c * PEER_CHUNK, PEER_CHUNK), PEER_CHUNK)
            e1 = e1_ref[rows, :]
            e2 = e2_ref[rows, :]
            g = g_ref[rows, :]
            g_hi = g.astype(BF16).astype(F32)
            gx = jnp.concatenate([g_hi, g - g_hi], axis=1)[:, None, :]
            e1x = jnp.concatenate([e1, e1], axis=1)[:, None, :]
            e2x = jnp.concatenate([e2, e2], axis=1)[:, None, :]
            a = jnp.where(e1x == key_id, 1.0, 0.0).astype(BF16)
            bm = jnp.where(e2x == key_id, gx, 0.0).astype(BF16)
            w = jnp.einsum("tep,tfp->tef", a, bm, preferred_element_type=F32)
            for i in range(PEER_CHUNK):
                base = pl.multiple_of((c * PEER_CHUNK + i) * WALL_PITCH, 8)
                wall_ref[pl.ds(base, PEER_KEYS), :] = w[i]
            return carry

        lax.fori_loop(0, tm // PEER_CHUNK, chunk, 0)

    nsub = eb // PEER_KEYS
    hpre = _mm_nt(x_ref[...], u_ref[...])
    w = jnp.concatenate([wall_ref[pl.ds(j * nsub + c, tm, stride=WALL_PITCH), :] for c in range(nsub)], axis=1)
    act = w * (0.5 * hpre * (1.0 + lax.erf(hpre * (2.0 ** -0.5))))
    o_ref[...] += _mm(act.astype(BF16), v_ref[...])

    @pl.when(j == pl.num_programs(1) - 1)
    def _():
        o_ref[...] = _layer_norm(r2_ref[...] + o_ref[...], g2_ref[...], b2_ref[...])


def _peer(h1b, e1, e2, g, u, v, r2, g2, b2, tm, eb):
    t = h1b.shape[0]
    n_exp = u.shape[0]
    npair = e1.shape[1]
    row = lambda w: pl.BlockSpec((tm, w), lambda i, j: (i, 0))
    tab = pl.BlockSpec((eb, D_MODEL), lambda i, j: (j, 0))
    vec = pl.BlockSpec((1, D_MODEL), lambda i, j: (0, 0))
    resid = pl.BlockSpec((tm, D_MODEL), lambda i, j: (i, 0), pipeline_mode=pl.Buffered(1))
    return pl.pallas_call(
        _peer_body,
        grid=(t // tm, n_exp // eb),
        in_specs=[row(D_MODEL), row(npair), row(npair), row(npair), tab, tab, resid, vec, vec],
        out_specs=row(D_MODEL),
        out_shape=jax.ShapeDtypeStruct((t, D_MODEL), F32),
        scratch_shapes=[pltpu.VMEM((tm * WALL_PITCH, PEER_KEYS), F32)],
        compiler_params=_params("parallel", "arbitrary"),
        name="peer",
    )(h1b, e1, e2, g, u, v, r2, g2, b2)


def _rope_tables(seq):
    t = jnp.arange(seq)
    row = (t // GRID_W).astype(F32)
    col = (t % GRID_W).astype(F32)
    axis_dim = MLA_ROPE // 2
    inv = ROPE_BASE ** (-jnp.arange(0, axis_dim, 2, dtype=F32) / axis_dim)
    ang = jnp.concatenate([row[:, None] * inv[None, :], col[:, None] * inv[None, :]], axis=-1)
    cos, sin = jnp.cos(ang), jnp.sin(ang)
    pad = LANES - MLA_NOPE - MLA_ROPE
    cos128 = jnp.concatenate([jnp.ones((seq, MLA_NOPE), F32), cos, cos, jnp.zeros((seq, pad), F32)], axis=1)
    sin128 = jnp.concatenate([jnp.zeros((seq, MLA_NOPE), F32), -sin, sin, jnp.zeros((seq, pad), F32)], axis=1)
    return cos128, sin128


_PERM = np.concatenate([np.arange(0, MLA_ROPE, 2), np.arange(1, MLA_ROPE, 2)])
_PERM_SW = np.concatenate([np.arange(1, MLA_ROPE, 2), np.arange(0, MLA_ROPE, 2)])


def _prep_weights(w_in, w_uq, w_ukv):
    pad = LANES - MLA_NOPE - MLA_ROPE
    zin = lambda n: jnp.zeros((D_MODEL, n), F32)
    kr = w_in[:, C_CKV:]
    w_in_ext = jnp.concatenate(
        [w_in[:, :C_CKV], zin(MLA_NOPE), kr[:, _PERM], zin(pad), zin(MLA_NOPE), kr[:, _PERM_SW], zin(pad)], axis=1)
    wq = w_uq.reshape(MLA_Q_RANK, MLA_HEADS, MLA_QD)
    zq = lambda n: jnp.zeros((MLA_Q_RANK, MLA_HEADS, n), F32)
    rope = wq[:, :, MLA_NOPE:]
    wq_main = jnp.concatenate([wq[:, :, :MLA_NOPE], rope[:, :, _PERM], zq(pad)], axis=2)
    wq_sw = jnp.concatenate([zq(MLA_NOPE), rope[:, :, _PERM_SW], zq(pad)], axis=2)
    wkv = w_ukv.reshape(MLA_KV_RANK, MLA_HEADS, MLA_NOPE + MLA_V)
    zk = jnp.zeros((MLA_KV_RANK, MLA_HEADS, LANES - MLA_NOPE), F32)
    wk_pad = jnp.concatenate([wkv[:, :, :MLA_NOPE], zk], axis=2)
    vv = wkv[:, :, MLA_NOPE:]
    zv = jnp.zeros_like(vv)
    odd = (jnp.arange(MLA_HEADS) % 2 == 1)[None, :, None]
    wv_pad = jnp.where(odd, jnp.concatenate([zv, vv], axis=2), jnp.concatenate([vv, zv], axis=2))
    hp = MLA_HEADS * LANES
    flat = lambda w: w.reshape(w.shape[0], hp).astype(BF16)
    return w_in_ext.astype(BF16), flat(wq_main), flat(wq_sw), flat(wk_pad), flat(wv_pad).T


def kernel(x, p, emb_ln_g, emb_ln_b, w_in, mla_q_norm_g, mla_kv_norm_g, w_uq, w_ukv, na_rpb, w_o, ln1_g, ln1_b,
           peer_w_q, peer_sub_keys, peer_u, peer_v, ple_w, ple_gate_w, ple_gate_b, ln2_g, ln2_b):
    b, seq, d = x.shape
    assert d == D_MODEL and seq % GRID_W == 0 and w_in.shape[0] == DEPTH
    t = b * seq
    vec = lambda a: a.reshape(1, -1).astype(F32)
    w_in_ext, wq_main, wq_sw, wk_pad, wv_pad_t = _prep_weights(w_in[0], w_uq[0], w_ukv[0])
    cos128, sin128 = _rope_tables(seq)

    h, qkv_na, q_mla, k_mla, v_mla_t = _proj(
        x.reshape(t, d), vec(emb_ln_g), vec(emb_ln_b), w_in_ext, vec(mla_q_norm_g[0]), vec(mla_kv_norm_g[0]),
        wq_main, wq_sw, wk_pad, wv_pad_t, cos128, sin128, seq, tm=PROJ_TM)

    bias = _na_bias(na_rpb[0].reshape(-1).astype(F32))
    a_na = _na_attn(qkv_na.reshape(b, seq, C_NA), bias, seq)
    hp = MLA_HEADS * LANES
    a_mla = _mla_attn(q_mla.reshape(b, seq, hp), k_mla.reshape(b, seq, hp), v_mla_t, tq=MLA_TQ)

    h1b, r2, e1, e2, gates = _mix(
        h, a_na.reshape(t, NA_WIDTH), a_mla.reshape(t, MLA_HEADS * MLA_V), p[0].reshape(t, PLE_DIM),
        w_o[0].astype(BF16), vec(ln1_g[0]), vec(ln1_b[0]), ple_gate_w[0].astype(BF16), vec(ple_gate_b[0]),
        ple_w[0].astype(BF16), peer_w_q[0].T.astype(BF16), peer_sub_keys[0].astype(BF16), tm=MIX_TM)

    out = _peer(h1b, e1, e2, gates, peer_u[0].astype(BF16), peer_v[0].astype(BF16), r2,
                vec(ln2_g[0]), vec(ln2_b[0]), tm=PEER_TM, eb=PEER_EB)
    return out.reshape(b, seq, d)
```
